```python
import math
import jax, jax.numpy as jnp
from jax import lax
import numpy as np

D_MODEL = 1024
BATCH = 2
SEQ = 16384
DEPTH = 4

N_A = DEPTH // 2
N_B = DEPTH - N_A
MEM_TOKENS = 256
MEM_HEADS = 4
MEM_DH = D_MODEL // 16
MEM_W = MEM_HEADS * MEM_DH
DN_DK = 128
DN_DV = 128
DN_HEADS = (3 * D_MODEL) // (4 * DN_DV)
DN_QK_W = DN_HEADS * DN_DK
DN_V_W = DN_HEADS * DN_DV
CONV_WIDTH = 4
CHUNK = 64
SWA_DH = 64
SWA_HEADS = (3 * D_MODEL) // (4 * SWA_DH)
SWA_KV_HEADS = 2
SWA_GROUP = SWA_HEADS // SWA_KV_HEADS
SWA_Q_W = SWA_HEADS * SWA_DH
SWA_KV_W = SWA_KV_HEADS * SWA_DH
WINDOW = 128
ROPE_THETA = 10000.0
MLP_HIDDEN = 4 * D_MODEL
LN_EPS = 1e-5
NORM_EPS = 1e-6
DN_ALPHA = (2.0 * DEPTH) ** 0.25
DN_BETA = (8.0 * DEPTH) ** -0.25
A_IN = 2 * DN_QK_W + 2 * DN_V_W + 2 * DN_HEADS + MEM_W
B_IN = SWA_Q_W + MEM_W
MIX_W = DN_V_W + MEM_W

kernel_name = "yoco_deltanet_swa_sink_memory_trunk"


def layer_norm(x, g, b):
    xf = x.astype(jnp.float32)
    mu = jnp.mean(xf, axis=-1, keepdims=True)
    var = jnp.mean(jnp.square(xf - mu), axis=-1, keepdims=True)
    y = (xf - mu) * lax.rsqrt(var + LN_EPS) * g.astype(jnp.float32) + b.astype(jnp.float32)
    return y.astype(x.dtype)


def l2_normalize(x):
    xf = x.astype(jnp.float32)
    return xf * lax.rsqrt(jnp.sum(xf * xf, axis=-1, keepdims=True) + NORM_EPS)


def rope_tables(positions, dh):
    inv_freq = ROPE_THETA ** (-jnp.arange(0, dh, 2, dtype=jnp.float32) / dh)
    ang = positions.astype(jnp.float32)[..., None] * inv_freq
    return jnp.cos(ang)[:, :, None, :], jnp.sin(ang)[:, :, None, :]


def apply_rope(x, cos, sin):
    xf = x.astype(jnp.float32)
    x1, x2 = jnp.split(xf, 2, axis=-1)
    out = jnp.concatenate([x1 * cos - x2 * sin, x2 * cos + x1 * sin], axis=-1)
    return out.astype(x.dtype)


def causal_depthwise_conv(x, w):
    C = x.shape[-1]
    return lax.conv_general_dilated(
        x, w[:, None, :].astype(x.dtype), window_strides=(1,),
        padding=[(CONV_WIDTH - 1, 0)], dimension_numbers=('NWC', 'WIO', 'NWC'),
        feature_group_count=C)


def gated_delta_rule(q, k, v, g, beta):
    B, S, H, DK = q.shape
    DV = v.shape[-1]
    N = S // CHUNK
    f32 = jnp.float32

    def to_chunks(t):
        t = t.astype(f32).reshape((B, N, CHUNK, H) + t.shape[3:])
        return jnp.moveaxis(t, 3, 1)

    q = to_chunks(q) * (DK ** -0.5)
    k = to_chunks(k)
    v = to_chunks(v)
    beta = to_chunks(beta)
    g = jnp.cumsum(to_chunks(g), axis=-1)
    incl = jnp.tril(jnp.ones((CHUNK, CHUNK), bool))
    strict = jnp.tril(jnp.ones((CHUNK, CHUNK), bool), -1)
    decay = jnp.exp(jnp.where(incl, g[..., :, None] - g[..., None, :], -jnp.inf))
    k_beta = k * beta[..., None]
    L = jnp.where(strict, jnp.einsum('bhnid,bhnjd->bhnij', k_beta, k) * decay, 0.0)
    rhs = jnp.concatenate([v * beta[..., None], k_beta * jnp.exp(g)[..., None]], axis=-1)
    sol = lax.linalg.triangular_solve(L, rhs, left_side=True, lower=True, unit_diagonal=True)
    u, w = sol[..., :DV], sol[..., DV:]
    intra = jnp.where(incl, jnp.einsum('bhnid,bhnjd->bhnij', q, k) * decay, 0.0)
    q_dec = q * jnp.exp(g)[..., None]
    k_dec = k * jnp.exp(g[..., -1:] - g)[..., None]
    chunk_decay = jnp.exp(g[..., -1])

    def step(state, inp):
        qd, kd, uc, wc, ac, cd = inp
        v_new = uc - jnp.einsum('bhik,bhkv->bhiv', wc, state)
        out = jnp.einsum('bhik,bhkv->bhiv', qd, state) + jnp.einsum('bhij,bhjv->bhiv', ac, v_new)
        state = state * cd[..., None, None] + jnp.einsum('bhik,bhiv->bhkv', kd, v_new)
        return state, out

    xs = (jnp.moveaxis(q_dec, 2, 0), jnp.moveaxis(k_dec, 2, 0), jnp.moveaxis(u, 2, 0),
          jnp.moveaxis(w, 2, 0), jnp.moveaxis(intra, 2, 0), jnp.moveaxis(chunk_decay, 2, 0))
    s0 = jnp.zeros((B, H, DK, DV), f32)
    _, out = lax.scan(step, s0, xs)
    return jnp.transpose(out, (1, 0, 3, 2, 4)).reshape(B, S, H, DV)


def sliding_window_sink_attention(q, k, v, sinks):
    B, S, HKV, G, dh = q.shape
    NB = S // WINDOW
    qb = q.reshape(B, NB, WINDOW, HKV, G, dh)

    def band_keys(t):
        tb = t.reshape(B, NB, WINDOW, HKV, dh)
        prev = jnp.pad(tb, ((0, 0), (1, 0), (0, 0), (0, 0), (0, 0)))[:, :-1]
        return jnp.concatenate([prev, tb], axis=2)

    kk = band_keys(k)
    vv = band_keys(v)
    s = jnp.einsum('bnqhgd,bnkhd->bnhgqk', qb, kk).astype(jnp.float32) * (dh ** -0.5)
    qi = jnp.arange(WINDOW)[:, None]
    kj = jnp.arange(2 * WINDOW)[None, :]
    diff = qi + WINDOW - kj
    band = (diff >= 0) & (diff < WINDOW)
    key_pos = jnp.arange(NB)[:, None] * WINDOW - WINDOW + kj
    valid = band[None] & (key_pos >= 0)[:, None, :]
    s = jnp.where(valid[None, :, None, None], s, -jnp.inf)
    sink = sinks.astype(jnp.float32).reshape(HKV, G)[None, None, :, :, None, None]
    m = jnp.maximum(jnp.max(s, axis=-1, keepdims=True), sink)
    p = jnp.exp(s - m)
    denom = jnp.sum(p, axis=-1, keepdims=True) + jnp.exp(sink - m)
    probs = (p / denom).astype(v.dtype)
    o = jnp.einsum('bnhgqk,bnkhd->bnqhgd', probs, vv)
    return o.reshape(B, S, HKV * G * dh)


def memory_cross_attention(qm, mem, w_kv):
    B, S, _ = qm.shape
    M = mem.shape[1]
    kv = mem @ w_kv
    k = kv[..., :MEM_W].reshape(B, M, MEM_HEADS, MEM_DH)
    v = kv[..., MEM_W:].reshape(B, M, MEM_HEADS, MEM_DH)
    q = qm.reshape(B, S, MEM_HEADS, MEM_DH)
    s = jnp.einsum('bshd,bmhd->bhsm', q, k).astype(jnp.float32) * (MEM_DH ** -0.5)
    p = jax.nn.softmax(s, axis=-1).astype(v.dtype)
    return jnp.einsum('bhsm,bmhd->bshd', p, v).reshape(B, S, MEM_W)


def mixer_a(h, mem, w_in, conv_w, A_log, dt_bias, norm_w, mem_w_kv, w_o):
    B, S, _ = h.shape
    proj = h @ w_in
    c1 = 2 * DN_QK_W + DN_V_W
    qkv = proj[..., :c1]
    z = proj[..., c1:c1 + DN_V_W]
    a = proj[..., c1 + DN_V_W:c1 + DN_V_W + DN_HEADS]
    b = proj[..., c1 + DN_V_W + DN_HEADS:c1 + DN_V_W + 2 * DN_HEADS]
    qm = proj[..., c1 + DN_V_W + 2 * DN_HEADS:]
    qkv = jax.nn.silu(causal_depthwise_conv(qkv, conv_w))
    q = l2_normalize(qkv[..., :DN_QK_W].reshape(B, S, DN_HEADS, DN_DK))
    k = l2_normalize(qkv[..., DN_QK_W:2 * DN_QK_W].reshape(B, S, DN_HEADS, DN_DK))
    v = qkv[..., 2 * DN_QK_W:].reshape(B, S, DN_HEADS, DN_DV)
    beta = jax.nn.sigmoid(b.astype(jnp.float32))
    g = -jnp.exp(A_log.astype(jnp.float32)) * jax.nn.softplus(a.astype(jnp.float32) + dt_bias.astype(jnp.float32))
    o = gated_delta_rule(q, k, v, g, beta)
    o = o * lax.rsqrt(jnp.mean(o * o, axis=-1, keepdims=True) + NORM_EPS) * norm_w.astype(jnp.float32)
    o = o * jax.nn.silu(z.astype(jnp.float32).reshape(B, S, DN_HEADS, DN_DV))
    o = o.astype(h.dtype).reshape(B, S, DN_V_W)
    mo = memory_cross_attention(qm, mem, mem_w_kv)
    return jnp.concatenate([o, mo], axis=-1) @ w_o


def mixer_b(h, mem, k_sh, v_sh, cos, sin, w_in, sinks, mem_w_kv, w_o):
    B, S, _ = h.shape
    proj = h @ w_in
    q = apply_rope(proj[..., :SWA_Q_W].reshape(B, S, SWA_HEADS, SWA_DH), cos, sin)
    q = q.reshape(B, S, SWA_KV_HEADS, SWA_GROUP, SWA_DH)
    o = sliding_window_sink_attention(q, k_sh, v_sh, sinks)
    mo = memory_cross_attention(proj[..., SWA_Q_W:], mem, mem_w_kv)
    return jnp.concatenate([o, mo], axis=-1) @ w_o


def shared_kv(h, w_kv, cos, sin):
    B, S, _ = h.shape
    kv = h @ w_kv
    k = apply_rope(kv[..., :SWA_KV_W].reshape(B, S, SWA_KV_HEADS, SWA_DH), cos, sin)
    v = kv[..., SWA_KV_W:].reshape(B, S, SWA_KV_HEADS, SWA_DH)
    return k, v


def sq_relu_mlp(h, w_up, w_down):
    return jnp.square(jax.nn.relu(h @ w_up)) @ w_down


def setup_inputs(seed: int = 0) -> dict:
    key = jax.random.key(seed)
    ks = jax.random.split(key, 20)
    f32 = jnp.float32

    def dense(k, shape, fan_in, scale=1.0):
        return jax.random.normal(k, shape, f32) * (fan_in ** -0.5) * scale

    x = jax.random.normal(ks[0], (BATCH, SEQ, D_MODEL), f32)
    mem = jax.random.normal(ks[1], (BATCH, MEM_TOKENS, D_MODEL), f32)
    positions = (jax.random.randint(ks[2], (BATCH, 1), 0, 4096, jnp.int32)
                 + jnp.arange(SEQ, dtype=jnp.int32)[None, :])
    a_w_in = dense(ks[3], (N_A, D_MODEL, A_IN), D_MODEL)
    a_conv_w = jax.random.normal(ks[4], (N_A, CONV_WIDTH, 2 * DN_QK_W + DN_V_W), f32) * (CONV_WIDTH ** -0.5)
    a_A_log = jnp.log(jax.random.uniform(ks[5], (N_A, DN_HEADS), f32, 1.0, 16.0))
    dt = jnp.exp(jax.random.uniform(ks[6], (N_A, DN_HEADS), f32, math.log(1e-3), math.log(1e-1)))
    a_dt_bias = dt + jnp.log(-jnp.expm1(-dt))
    a_norm_w = 1.0 + 0.02 * jax.random.normal(ks[7], (N_A, DN_DV), f32)
    b_w_in = dense(ks[8], (N_B, D_MODEL, B_IN), D_MODEL)
    b_sinks = 0.5 * jax.random.normal(ks[9], (N_B, SWA_HEADS), f32)
    w_kv_shared = dense(ks[10], (D_MODEL, 2 * SWA_KV_W), D_MODEL)
    mem_w_kv = dense(ks[11], (DEPTH, D_MODEL, 2 * MEM_W), D_MODEL)
    w_o = dense(ks[12], (DEPTH, MIX_W, D_MODEL), MIX_W, DN_BETA)
    mlp_w_up = dense(ks[13], (DEPTH, D_MODEL, MLP_HIDDEN), D_MODEL)
    mlp_w_down = dense(ks[14], (DEPTH, MLP_HIDDEN, D_MODEL), MLP_HIDDEN, DN_BETA)
    ln_g = 1.0 + 0.02 * jax.random.normal(ks[15], (DEPTH, 2, D_MODEL), f32)
    ln_b = 0.02 * jax.random.normal(ks[16], (DEPTH, 2, D_MODEL), f32)
    return {"x": x, "mem": mem, "positions": positions, "a_w_in": a_w_in, "a_conv_w": a_conv_w,
            "a_A_log": a_A_log, "a_dt_bias": a_dt_bias, "a_norm_w": a_norm_w, "b_w_in": b_w_in,
            "b_sinks": b_sinks, "w_kv_shared": w_kv_shared, "mem_w_kv": mem_w_kv, "w_o": w_o,
            "mlp_w_up": mlp_w_up, "mlp_w_down": mlp_w_down, "ln_g": ln_g, "ln_b": ln_b}


def reference(x, mem, positions, a_w_in, a_conv_w, a_A_log, a_dt_bias, a_norm_w, b_w_in,
              b_sinks, w_kv_shared, mem_w_kv, w_o, mlp_w_up, mlp_w_down, ln_g, ln_b):
    cos, sin = rope_tables(positions, SWA_DH)
    h = x
    k_sh = None
    v_sh = None
    for layer in range(DEPTH):
        if layer < N_A:
            mix = mixer_a(h, mem, a_w_in[layer], a_conv_w[layer], a_A_log[layer], a_dt_bias[layer],
                          a_norm_w[layer], mem_w_kv[layer], w_o[layer])
        else:
            j = layer - N_A
            mix = mixer_b(h, mem, k_sh, v_sh, cos, sin, b_w_in[j], b_sinks[j], mem_w_kv[layer], w_o[layer])
        h = layer_norm(DN_ALPHA * h + mix, ln_g[layer, 0], ln_b[layer, 0])
        h = layer_norm(DN_ALPHA * h + sq_relu_mlp(h, mlp_w_up[layer], mlp_w_down[layer]),
                       ln_g[layer, 1], ln_b[layer, 1])
        if layer == N_A - 1:
            k_sh, v_sh = shared_kv(h, w_kv_shared, cos, sin)
    return h
```

```python
import functools

import jax
import jax.numpy as jnp
from jax import lax
from jax.experimental import pallas as pl
from jax.experimental.pallas import tpu as pltpu

F32 = jnp.float32
BF16 = jnp.bfloat16

V7X_LANES = 128
V7X_VMEM_LIMIT_BYTES = 56 * 1024 * 1024

DEPTH = 4
N_A = DEPTH // 2
MEM_HEADS = 4
MEM_DH = 64
MEM_W = MEM_HEADS * MEM_DH
DN_DK = 128
DN_HEADS = 6
DN_PAIRS = DN_HEADS // 2
DN_W = DN_HEADS * DN_DK
CONV_WIDTH = 4
CONV_HALO = 16
CHUNK = 64
SWA_DH = 64
SWA_HEADS = 12
SWA_KV_HEADS = 2
SWA_Q_W = SWA_HEADS * SWA_DH
SWA_PAIRS = SWA_Q_W // V7X_LANES
SWA_PAIRS_PER_KV = SWA_PAIRS // SWA_KV_HEADS
WINDOW = 128
ROPE_THETA = 10000.0
LN_EPS = 1e-5
NORM_EPS = 1e-6
DN_ALPHA = (2.0 * DEPTH) ** 0.25

ROW_TILE = 512
GDN_TILE = 256
SWA_TILE = 512


def _cparams(n_grid_axes):
    return pltpu.CompilerParams(
        dimension_semantics=("arbitrary",) * n_grid_axes,
        vmem_limit_bytes=V7X_VMEM_LIMIT_BYTES)


def _mm(a, b):
    return jnp.dot(a.astype(BF16), b.astype(BF16), preferred_element_type=F32)


def _mm_nt(a, b):
    return lax.dot_general(a.astype(BF16), b.astype(BF16), (((1,), (1,)), ((), ())),
                           preferred_element_type=F32)


def _mm_f32(a, b):
    return lax.dot_general(a, b, (((1,), (0,)), ((), ())), precision=lax.Precision.HIGHEST,
                           preferred_element_type=F32)


def _sigmoid(x):
    return 1.0 / (1.0 + jnp.exp(-x))


def _layer_norm(y, g, b):
    mu = jnp.mean(y, axis=-1, keepdims=True)
    d = y - mu
    var = jnp.mean(d * d, axis=-1, keepdims=True)
    return d * lax.rsqrt(var + LN_EPS) * g + b


def _proj_kernel(x_ref, w_ref, *out_refs, segs, col_chunk):
    xb = x_ref[...].astype(BF16)
    for o_ref, (start, width) in zip(out_refs, segs):
        for c0 in range(0, width, col_chunk):
            c1 = min(c0 + col_chunk, width)
            acc = jnp.dot(xb, w_ref[:, start + c0:start + c1], preferred_element_type=F32)
            o_ref[:, c0:c1] = acc.astype(o_ref.dtype)


def _proj(x, w, segs, dtypes, tm):
    m, k = x.shape
    n = w.shape[1]
    assert m % tm == 0
    return pl.pallas_call(
        functools.partial(_proj_kernel, segs=segs, col_chunk=512),
        grid=(m // tm,),
        in_specs=[pl.BlockSpec((tm, k), lambda i: (i, 0)),
                  pl.BlockSpec((k, n), lambda i: (0, 0))],
        out_specs=[pl.BlockSpec((tm, wd), lambda i: (i, 0)) for _, wd in segs],
        out_shape=[jax.ShapeDtypeStruct((m, wd), dt) for (_, wd), dt in zip(segs, dtypes)],
        compiler_params=_cparams(1),
        name="proj",
    )(x, w)


def _rope_table_kernel(pos_ref, inv_ref, cos_ref, sin_ref):
    ang = pos_ref[...].astype(F32) * inv_ref[...]
    cos_ref[...] = jnp.cos(ang)
    sin_ref[...] = jnp.sin(ang)


def _rope_tables(positions):
    b, s = positions.shape
    half = SWA_DH // 2
    inv_freq = ROPE_THETA ** (-jnp.arange(0, SWA_DH, 2, dtype=F32) / SWA_DH)
    reps = V7X_LANES // half
    pos_rep = jnp.broadcast_to(positions[..., None], (b, s, half)).reshape(b * s // reps, V7X_LANES)
    inv_rep = jnp.tile(inv_freq, reps)[None, :]
    rows = pos_rep.shape[0]
    tr = min(rows, 2048)
    assert rows % tr == 0
    cos_d, sin_d = pl.pallas_call(
        _rope_table_kernel,
        grid=(rows // tr,),
        in_specs=[pl.BlockSpec((tr, V7X_LANES), lambda i: (i, 0)),
                  pl.BlockSpec((1, V7X_LANES), lambda i: (0, 0))],
        out_specs=[pl.BlockSpec((tr, V7X_LANES), lambda i: (i, 0))] * 2,
        out_shape=[jax.ShapeDtypeStruct((rows, V7X_LANES), F32)] * 2,
        compiler_params=_cparams(1),
        name="rope_tables",
    )(pos_rep, inv_rep)
    cos32 = cos_d.reshape(b, s, half)
    sin32 = sin_d.reshape(b, s, half)
    cos_t = jnp.tile(cos32, (1, 1, reps))
    sin_t = jnp.tile(jnp.concatenate([-sin32, sin32], axis=-1), (1, 1, reps // 2))
    return cos_t, sin_t


def _rope_pairs(x, cos_t, sin_t):
    lane = lax.broadcasted_iota(jnp.int32, x.shape, 1)
    first_half = (lane & (SWA_DH - 1)) < (SWA_DH // 2)
    swapped = jnp.where(first_half,
                        pltpu.roll(x, V7X_LANES - SWA_DH // 2, axis=1),
                        pltpu.roll(x, SWA_DH // 2, axis=1))
    return x * cos_t + swapped * sin_t


def _gdn_prep_kernel(x_ref, halo_ref, ab_ref, cw_ref, alog_ref, dtb_ref,
                     qkv_ref, beta_ref, gc_ref, gr_ref, xs_ref):
    ts = x_ref.shape[1]
    not_first = jnp.where(pl.program_id(1) > 0, 1.0, 0.0)
    xs_ref[0:CONV_HALO, :] = halo_ref[0].astype(F32) * not_first
    xs_ref[CONV_HALO:CONV_HALO + ts, :] = x_ref[0].astype(F32)

    n_blocks = (3 * DN_W) // V7X_LANES
    for blk in range(n_blocks):
        c0 = blk * V7X_LANES
        y = jnp.zeros((ts, V7X_LANES), F32)
        for j in range(CONV_WIDTH):
            r0 = CONV_HALO - (CONV_WIDTH - 1) + j
            y = y + cw_ref[j:j + 1, c0:c0 + V7X_LANES] * xs_ref[r0:r0 + ts, c0:c0 + V7X_LANES]
        y = y * _sigmoid(y)
        if blk < 2 * DN_HEADS:
            y = y * lax.rsqrt(jnp.sum(y * y, axis=-1, keepdims=True) + NORM_EPS)
        qkv_ref[0, :, c0:c0 + V7X_LANES] = y.astype(qkv_ref.dtype)

    ab = ab_ref[0]
    a = ab[:, :V7X_LANES]
    b = ab[:, V7X_LANES:]
    xg = a + dtb_ref[...]
    softplus = jnp.maximum(xg, 0.0) + jnp.log1p(jnp.exp(-jnp.abs(xg)))
    g = -jnp.exp(alog_ref[...]) * softplus
    beta_ref[0] = _sigmoid(b)

    row = lax.broadcasted_iota(jnp.int32, g.shape, 0) & (CHUNK - 1)
    gc = g
    step = 1
    while step < CHUNK:
        gc = gc + jnp.where(row >= step, pltpu.roll(gc, step, axis=0), 0.0)
        step *= 2
    gc_ref[0] = gc

    gct = jnp.transpose(gc)[0:8, :]
    plus = pltpu.roll(pltpu.roll(gct, CHUNK, axis=1), 7, axis=0)
    minus = pltpu.roll(pltpu.roll(gct, ts - CHUNK, axis=1), 1, axis=0)
    lane = lax.broadcasted_iota(jnp.int32, gct.shape, 1)
    low = (lane & (V7X_LANES - 1)) < CHUNK
    even = (lax.broadcasted_iota(jnp.int32, gct.shape, 0) & 1) == 0
    gr_ref[0] = jnp.where(even, jnp.where(low, gct, plus), jnp.where(low, minus, gct))


def _gdn_prep(qkv_pre, ab, conv_w, a_log, dt_bias, ts):
    b, s, w = qkv_pre.shape
    assert s % ts == 0 and ts % V7X_LANES == 0
    halo_blocks = ts // CONV_HALO
    pad = V7X_LANES - DN_HEADS
    alog = jnp.pad(a_log.astype(F32), (0, pad))[None, :]
    dtb = jnp.pad(dt_bias.astype(F32), (0, pad))[None, :]
    return pl.pallas_call(
        _gdn_prep_kernel,
        grid=(b, s // ts),
        in_specs=[
            pl.BlockSpec((1, ts, w), lambda bi, i: (bi, i, 0)),
            pl.BlockSpec((1, CONV_HALO, w), lambda bi, i: (bi, jnp.maximum(i * halo_blocks - 1, 0), 0)),
            pl.BlockSpec((1, ts, 2 * V7X_LANES), lambda bi, i: (bi, i, 0)),
            pl.BlockSpec((CONV_WIDTH, w), lambda bi, i: (0, 0)),
            pl.BlockSpec((1, V7X_LANES), lambda bi, i: (0, 0)),
            pl.BlockSpec((1, V7X_LANES), lambda bi, i: (0, 0)),
        ],
        out_specs=[
            pl.BlockSpec((1, ts, w), lambda bi, i: (bi, i, 0)),
            pl.BlockSpec((1, ts, V7X_LANES), lambda bi, i: (bi, i, 0)),
            pl.BlockSpec((1, ts, V7X_LANES), lambda bi, i: (bi, i, 0)),
            pl.BlockSpec((1, 8, ts), lambda bi, i: (bi, 0, i)),
        ],
        out_shape=[
            jax.ShapeDtypeStruct((b, s, w), BF16),
            jax.ShapeDtypeStruct((b, s, V7X_LANES), F32),
            jax.ShapeDtypeStruct((b, s, V7X_LANES), F32),
            jax.ShapeDtypeStruct((b, 8, s), F32),
        ],
        scratch_shapes=[pltpu.VMEM((CONV_HALO + ts, w), F32)],
        compiler_params=_cparams(2),
        name="gdn_prep",
    )(qkv_pre, qkv_pre, ab, conv_w.astype(F32), alog, dtb)


def _gdn_kernel(qkv_ref, z_ref, beta_ref, gc_ref, gr_ref, nw_ref, o_ref, state_ref):
    ts = qkv_ref.shape[1]
    c = CHUNK
    lanes = V7X_LANES

    @pl.when(pl.program_id(1) == 0)
    def _():
        state_ref[...] = jnp.zeros_like(state_ref)

    lane2 = lax.broadcasted_iota(jnp.int32, (c, 2 * lanes), 1)
    head_a2 = lane2 < lanes
    lane1 = lax.broadcasted_iota(jnp.int32, (c, lanes), 1)
    half_a = lane1 < c
    col = lane1 & (c - 1)
    row = lax.broadcasted_iota(jnp.int32, (c, lanes), 0)
    incl = row >= col
    strict = row > col
    eye = jnp.where(row == col, 1.0, 0.0)
    lane_sq = lax.broadcasted_iota(jnp.int32, (lanes, lanes), 1)
    half_a_sq = lane_sq < c
    nw = nw_ref[...]
    q_scale = DN_DK ** -0.5

    def split_halves(x, mask):
        return jnp.concatenate([jnp.where(mask, x, 0.0), jnp.where(mask, 0.0, x)], axis=0)

    for p in range(DN_PAIRS):
        ha, hb = 2 * p, 2 * p + 1
        s_a = state_ref[ha]
        s_b = state_ref[hb]
        for ci in range(ts // c):
            r0 = ci * c
            qp = qkv_ref[0, r0:r0 + c, 2 * lanes * p:2 * lanes * (p + 1)].astype(F32)
            kp = qkv_ref[0, r0:r0 + c, DN_W + 2 * lanes * p:DN_W + 2 * lanes * (p + 1)].astype(F32)
            vp = qkv_ref[0, r0:r0 + c, 2 * DN_W + 2 * lanes * p:2 * DN_W + 2 * lanes * (p + 1)].astype(F32)
            beta_t = beta_ref[0, r0:r0 + c, :]
            gc_t = gc_ref[0, r0:r0 + c, :]
            b_a, b_b = beta_t[:, ha:ha + 1], beta_t[:, hb:hb + 1]
            g_a, g_b = gc_t[:, ha:ha + 1], gc_t[:, hb:hb + 1]
            beta2 = jnp.where(head_a2, b_a, b_b)
            g2 = jnp.where(head_a2, g_a, g_b)
            g1 = jnp.where(half_a, g_a, g_b)
            g_row = gr_ref[0, 2 * p + (ci % 2):2 * p + (ci % 2) + 1, lanes * (ci // 2):lanes * (ci // 2 + 1)]
            g_last = g2[c - 1:c, :]
            eg = jnp.exp(g2)
            kb = kp * beta2
            vb = vp * beta2
            kbe = kb * eg
            qs = qp * q_scale
            qe = qs * eg
            kd = kp * jnp.exp(g_last - g2)
            cd = jnp.exp(g_last)

            r1 = _mm_nt(jnp.concatenate([kb, qs], axis=0), split_halves(kp, head_a2))
            decay = jnp.where(incl, jnp.exp(g1 - g_row), 0.0)
            x = jnp.where(strict, -(r1[:c] * decay), 0.0)
            intra = r1[c:] * decay

            inv = eye + x
            x = _mm_f32(x, split_halves(x, half_a))
            n_sq = c.bit_length() - 1
            for _ in range(n_sq - 2):
                r = _mm_f32(jnp.concatenate([x, inv], axis=0), split_halves(x, half_a))
                x = r[:c]
                inv = inv + r[c:]
            inv = inv + _mm_f32(inv, split_halves(x, half_a))

            rhs = jnp.concatenate([
                jnp.concatenate([vb[:, :lanes], kbe[:, :lanes]], axis=1),
                jnp.concatenate([vb[:, lanes:], kbe[:, lanes:]], axis=1)], axis=0)
            sol = _mm_f32(split_halves(inv, half_a), rhs)

            kdt = jnp.transpose(jnp.concatenate([kd[:, :lanes], kd[:, lanes:]], axis=0))
            r3 = _mm(jnp.concatenate([split_halves(intra, half_a), split_halves(kdt, half_a_sq)], axis=0),
                     sol)

            outs = []
            new_states = []
            for hx, (s_x, lo) in enumerate(((s_a, 0), (s_b, lanes))):
                o0 = r3[hx * c:(hx + 1) * c, :lanes]
                iw = r3[hx * c:(hx + 1) * c, lanes:]
                n0 = r3[2 * c + hx * lanes:2 * c + (hx + 1) * lanes, :lanes]
                kw = r3[2 * c + hx * lanes:2 * c + (hx + 1) * lanes, lanes:]
                r4 = _mm(jnp.concatenate([qe[:, lo:lo + lanes] - iw, -kw], axis=0), s_x)
                outs.append(r4[:c] + o0)
                new_states.append(cd[:, lo:lo + lanes] * s_x + n0 + r4[c:])
            s_a, s_b = new_states

            for hx, o in enumerate(outs):
                hcol = (2 * p + hx) * lanes
                zt = z_ref[0, r0:r0 + c, hcol:hcol + lanes].astype(F32)
                on = o * lax.rsqrt(jnp.mean(o * o, axis=-1, keepdims=True) + NORM_EPS) * nw
                o_ref[0, r0:r0 + c, hcol:hcol + lanes] = (on * (zt * _sigmoid(zt))).astype(o_ref.dtype)
        state_ref[ha] = s_a
        state_ref[hb] = s_b


def _gdn(qkv, z, beta, gc, gr, norm_w, ts):
    b, s, w = qkv.shape
    return pl.pallas_call(
        _gdn_kernel,
        grid=(b, s // ts),
        in_specs=[
            pl.BlockSpec((1, ts, w), lambda bi, i: (bi, i, 0)),
            pl.BlockSpec((1, ts, DN_W), lambda bi, i: (bi, i, 0)),
            pl.BlockSpec((1, ts, V7X_LANES), lambda bi, i: (bi, i, 0)),
            pl.BlockSpec((1, ts, V7X_LANES), lambda bi, i: (bi, i, 0)),
            pl.BlockSpec((1, 8, ts), lambda bi, i: (bi, 0, i)),
            pl.BlockSpec((1, V7X_LANES), lambda bi, i: (0, 0)),
        ],
        out_specs=pl.BlockSpec((1, ts, DN_W), lambda bi, i: (bi, i, 0)),
        out_shape=jax.ShapeDtypeStruct((b, s, DN_W), BF16),
        scratch_shapes=[pltpu.VMEM((DN_HEADS, DN_DK, DN_DK), F32)],
        compiler_params=_cparams(2),
        name="gdn",
    )(qkv, z, beta, gc, gr, norm_w.astype(F32)[None, :])


def _swa_kernel(sink_ref, q_ref, kv_ref, kvh_ref, cos_ref, sin_ref, cosh_ref, sinh_ref, o_ref,
                kvar_ref, vvar_ref):
    tq = q_ref.shape[1]
    lanes = V7X_LANES
    w = WINDOW
    first_key = jnp.where(pl.program_id(1) == 0, w, 0)

    kv_all = jnp.concatenate([kvh_ref[0], kv_ref[0]], axis=0)
    cos_all = jnp.concatenate([cosh_ref[0], cos_ref[0]], axis=0)
    sin_all = jnp.concatenate([sinh_ref[0], sin_ref[0]], axis=0)
    k_rot = _rope_pairs(kv_all[:, :lanes], cos_all, sin_all)
    v_all = kv_all[:, lanes:]
    lane = lax.broadcasted_iota(jnp.int32, k_rot.shape, 1)
    low = lane < SWA_DH
    for src, dst in ((k_rot, kvar_ref), (v_all, vvar_ref)):
        swapped = pltpu.roll(src, SWA_DH, axis=1)
        dst[0] = jnp.where(low, src, 0.0).astype(BF16)
        dst[1] = jnp.where(low, 0.0, swapped).astype(BF16)
        dst[2] = jnp.where(low, swapped, 0.0).astype(BF16)
        dst[3] = jnp.where(low, 0.0, src).astype(BF16)

    scale = SWA_DH ** -0.5
    q_rot = []
    for p in range(SWA_PAIRS):
        qf = q_ref[0, :, p * lanes:(p + 1) * lanes].astype(F32)
        q_rot.append((_rope_pairs(qf, cos_ref[0], sin_ref[0]) * scale).astype(BF16))

    rows3 = SWA_PAIRS_PER_KV * w
    qi = lax.broadcasted_iota(jnp.int32, (rows3, 2 * w), 0) & (w - 1)
    kj = lax.broadcasted_iota(jnp.int32, (rows3, 2 * w), 1)
    dist = qi + w - kj
    band = (dist >= 0) & (dist < w)
    head_slot = lax.broadcasted_iota(jnp.int32, (rows3, 1), 0) // w
    neg_inf = jnp.float32(-jnp.inf)

    for j in range(tq // w):
        r0 = j * w
        if j == 0:
            valid = band & (kj >= first_key)
        else:
            valid = band
        for g in range(SWA_KV_HEADS):
            pairs = [SWA_PAIRS_PER_KV * g + t for t in range(SWA_PAIRS_PER_KV)]
            qs = jnp.concatenate([q_rot[p][r0:r0 + w] for p in pairs], axis=0)
            acc = jnp.zeros((rows3, lanes), F32)
            for e in range(2):
                kw_ = kvar_ref[2 * g + e, r0:r0 + 2 * w, :]
                vw_ = vvar_ref[2 * g + e, r0:r0 + 2 * w, :]
                s = lax.dot_general(qs, kw_, (((1,), (1,)), ((), ())), preferred_element_type=F32)
                s = jnp.where(valid, s, neg_inf)
                heads = [2 * p + e for p in pairs]
                sink = jnp.where(head_slot == 0, sink_ref[heads[0]],
                                 jnp.where(head_slot == 1, sink_ref[heads[1]], sink_ref[heads[2]]))
                m = jnp.maximum(jnp.max(s, axis=-1, keepdims=True), sink)
                pe = jnp.exp(s - m)
                den = jnp.sum(pe, axis=-1, keepdims=True) + jnp.exp(sink - m)
                probs = (pe * (1.0 / den)).astype(BF16)
                acc = acc + jnp.dot(probs, vw_, preferred_element_type=F32)
            for t, p in enumerate(pairs):
                o_ref[0, r0:r0 + w, p * lanes:(p + 1) * lanes] = acc[t * w:(t + 1) * w].astype(o_ref.dtype)


def _swa(q_pre, kv_sh, cos_t, sin_t, sinks, tq):
    b, s, _ = q_pre.shape
    assert s % tq == 0 and tq % WINDOW == 0
    hb = tq // WINDOW
    cur = lambda bi, i: (bi, i, 0)
    halo = lambda bi, i: (bi, jnp.maximum(i * hb - 1, 0), 0)
    lanes = V7X_LANES
    return pl.pallas_call(
        _swa_kernel,
        grid=(b, s // tq),
        in_specs=[
            pl.BlockSpec(memory_space=pltpu.SMEM),
            pl.BlockSpec((1, tq, SWA_Q_W), cur),
            pl.BlockSpec((1, tq, 2 * lanes), cur),
            pl.BlockSpec((1, WINDOW, 2 * lanes), halo),
            pl.BlockSpec((1, tq, lanes), cur),
            pl.BlockSpec((1, tq, lanes), cur),
            pl.BlockSpec((1, WINDOW, lanes), halo),
            pl.BlockSpec((1, WINDOW, lanes), halo),
        ],
        out_specs=pl.BlockSpec((1, tq, SWA_Q_W), cur),
        out_shape=jax.ShapeDtypeStruct((b, s, SWA_Q_W), BF16),
        scratch_shapes=[pltpu.VMEM((4, WINDOW + tq, lanes), BF16),
                        pltpu.VMEM((4, WINDOW + tq, lanes), BF16)],
        compiler_params=_cparams(2),
        name="swa",
    )(sinks.astype(F32), q_pre, kv_sh, kv_sh, cos_t, sin_t, cos_t, sin_t)


def _mix_out_kernel(o_ref, qm_ref, mk_ref, mv_ref, wo_ref, h_ref, g_ref, b_ref, out_ref):
    qm = qm_ref[0]
    mk = mk_ref[0].astype(F32)
    mv = mv_ref[0].astype(F32)
    lane_head = lax.broadcasted_iota(jnp.int32, mk.shape, 1) // MEM_DH
    scale = MEM_DH ** -0.5
    mo = jnp.zeros(qm.shape, F32)
    for hd in range(MEM_HEADS):
        kh = jnp.where(lane_head == hd, mk, 0.0).astype(BF16)
        vh = jnp.where(lane_head == hd, mv, 0.0).astype(BF16)
        s = lax.dot_general(qm, kh, (((1,), (1,)), ((), ())), preferred_element_type=F32) * scale
        m = jnp.max(s, axis=-1, keepdims=True)
        pe = jnp.exp(s - m)
        probs = pe * (1.0 / jnp.sum(pe, axis=-1, keepdims=True))
        mo = mo + jnp.dot(probs.astype(BF16), vh, preferred_element_type=F32)
    dn = o_ref.shape[2]
    mix = (jnp.dot(o_ref[0], wo_ref[0:dn, :], preferred_element_type=F32)
           + jnp.dot(mo.astype(BF16), wo_ref[dn:, :], preferred_element_type=F32))
    y = DN_ALPHA * h_ref[0] + mix
    out_ref[0] = _layer_norm(y, g_ref[...], b_ref[...])


def _mix_out(o, qm, mk, mv, wo, h, ln_g, ln_b, tm):
    b, s, d = h.shape
    m_tok = mk.shape[1]
    cur = lambda bi, i: (bi, i, 0)
    per_b = lambda bi, i: (bi, 0, 0)
    const = lambda bi, i: (0, 0)
    return pl.pallas_call(
        _mix_out_kernel,
        grid=(b, s // tm),
        in_specs=[
            pl.BlockSpec((1, tm, o.shape[2]), cur),
            pl.BlockSpec((1, tm, MEM_W), cur),
            pl.BlockSpec((1, m_tok, MEM_W), per_b),
            pl.BlockSpec((1, m_tok, MEM_W), per_b),
            pl.BlockSpec(wo.shape, const),
            pl.BlockSpec((1, tm, d), cur),
            pl.BlockSpec((1, d), const),
            pl.BlockSpec((1, d), const),
        ],
        out_specs=pl.BlockSpec((1, tm, d), cur),
        out_shape=jax.ShapeDtypeStruct((b, s, d), F32),
        compiler_params=_cparams(2),
        name="mix_out",
    )(o, qm, mk, mv, wo, h, ln_g[None, :], ln_b[None, :])


def _mlp_kernel(x_ref, up_ref, dn_ref, g_ref, b_ref, o_ref, *, hid_chunk):
    x = x_ref[...]
    xb = x.astype(BF16)
    hidden = up_ref.shape[1]
    acc = jnp.zeros(x.shape, F32)
    for j in range(0, hidden, hid_chunk):
        hid = jnp.dot(xb, up_ref[:, j:j + hid_chunk], preferred_element_type=F32)
        hid = jnp.square(jnp.maximum(hid, 0.0))
        acc = acc + jnp.dot(hid.astype(BF16), dn_ref[j:j + hid_chunk, :], preferred_element_type=F32)
    o_ref[...] = _layer_norm(DN_ALPHA * x + acc, g_ref[...], b_ref[...])


def _mlp(x, w_up, w_down, ln_g, ln_b, tm):
    m, d = x.shape
    hidden = w_up.shape[1]
    return pl.pallas_call(
        functools.partial(_mlp_kernel, hid_chunk=512),
        grid=(m // tm,),
        in_specs=[
            pl.BlockSpec((tm, d), lambda i: (i, 0)),
            pl.BlockSpec((d, hidden), lambda i: (0, 0), pipeline_mode=pl.Buffered(1)),
            pl.BlockSpec((hidden, d), lambda i: (0, 0), pipeline_mode=pl.Buffered(1)),
            pl.BlockSpec((1, d), lambda i: (0, 0)),
            pl.BlockSpec((1, d), lambda i: (0, 0)),
        ],
        out_specs=pl.BlockSpec((tm, d), lambda i: (i, 0)),
        out_shape=jax.ShapeDtypeStruct((m, d), F32),
        compiler_params=_cparams(1),
        name="mlp",
    )(x, w_up, w_down, ln_g[None, :], ln_b[None, :])


def _a_proj_weights(w_in):
    c1 = 3 * DN_W
    z_end = c1 + DN_W
    a_end = z_end + DN_HEADS
    b_end = a_end + DN_HEADS
    pad = V7X_LANES - DN_HEADS
    return jnp.concatenate([
        w_in[:, :z_end], w_in[:, b_end:],
        jnp.pad(w_in[:, z_end:a_end], ((0, 0), (0, pad))),
        jnp.pad(w_in[:, a_end:b_end], ((0, 0), (0, pad)))], axis=1).astype(BF16)


def _tile(n, pref):
    t = min(n, pref)
    assert n % t == 0
    return t


def kernel(x, mem, positions, a_w_in, a_conv_w, a_A_log, a_dt_bias, a_norm_w, b_w_in, b_sinks,
           w_kv_shared, mem_w_kv, w_o, mlp_w_up, mlp_w_down, ln_g, ln_b):
    b, s, d = x.shape
    t = b * s
    m_tok = mem.shape[1]
    tm = _tile(s, ROW_TILE)
    mem2 = mem.reshape(b * m_tok, d)
    h = x
    kv_sh = None
    cos_t = sin_t = None
    a_segs = ((0, 3 * DN_W), (3 * DN_W, DN_W), (4 * DN_W, MEM_W), (4 * DN_W + MEM_W, 2 * V7X_LANES))
    for layer in range(DEPTH):
        mk, mv = _proj(mem2, mem_w_kv[layer].astype(BF16), ((0, MEM_W), (MEM_W, MEM_W)), (BF16, BF16),
                       b * m_tok)
        mk = mk.reshape(b, m_tok, MEM_W)
        mv = mv.reshape(b, m_tok, MEM_W)
        h2 = h.reshape(t, d)
        if layer < N_A:
            qkv_pre, z, qm, ab = _proj(h2, _a_proj_weights(a_w_in[layer]), a_segs, (BF16, BF16, BF16, F32), tm)
            ts = _tile(s, GDN_TILE)
            qkv, beta, gc, gr = _gdn_prep(qkv_pre.reshape(b, s, -1), ab.reshape(b, s, -1), a_conv_w[layer],
                                          a_A_log[layer], a_dt_bias[layer], ts)
            o = _gdn(qkv, z.reshape(b, s, -1), beta, gc, gr, a_norm_w[layer], ts)
        else:
            j = layer - N_A
            q_pre, qm = _proj(h2, b_w_in[j].astype(BF16), ((0, SWA_Q_W), (SWA_Q_W, MEM_W)), (BF16, BF16), tm)
            o = _swa(q_pre.reshape(b, s, -1), kv_sh, cos_t, sin_t, b_sinks[j], _tile(s, SWA_TILE))
        h = _mix_out(o, qm.reshape(b, s, -1), mk, mv, w_o[layer].astype(BF16), h,
                     ln_g[layer, 0], ln_b[layer, 0], tm)
        h = _mlp(h.reshape(t, d), mlp_w_up[layer].astype(BF16), mlp_w_down[layer].astype(BF16),
                 ln_g[layer, 1], ln_b[layer, 1], tm).reshape(b, s, d)
        if layer == N_A - 1:
            (kv_pre,) = _proj(h.reshape(t, d), w_kv_shared.astype(BF16), ((0, 2 * V7X_LANES),), (F32,), tm)
            kv_sh = kv_pre.reshape(b, s, -1)
            cos_t, sin_t = _rope_tables(positions)
    return h
```

```python
import functools

import jax
import jax.numpy as jnp
from jax import lax
from jax.experimental import pallas as pl
from jax.experimental.pallas import tpu as pltpu

F32 = jnp.float32
BF16 = jnp.bfloat16

V7X_LANES = 128
V7X_VMEM_LIMIT_BYTES = 56 * 1024 * 1024

DEPTH = 4
N_A = DEPTH // 2
MEM_HEADS = 4
MEM_DH = 64
MEM_W = MEM_HEADS * MEM_DH
DN_DK = 128
DN_HEADS = 6
DN_PAIRS = DN_HEADS // 2
DN_W = DN_HEADS * DN_DK
CONV_WIDTH = 4
CONV_HALO = 16
CHUNK = 64
SWA_DH = 64
SWA_HEADS = 12
SWA_KV_HEADS = 2
SWA_Q_W = SWA_HEADS * SWA_DH
SWA_PAIRS = SWA_Q_W // V7X_LANES
SWA_PAIRS_PER_KV = SWA_PAIRS // SWA_KV_HEADS
WINDOW = 128
ROPE_THETA = 10000.0
LN_EPS = 1e-5
NORM_EPS = 1e-6
DN_ALPHA = (2.0 * DEPTH) ** 0.25

ROW_TILE = 512
GDN_TILE = 256
SWA_TILE = 512


def _cparams(n_grid_axes):
    return pltpu.CompilerParams(
        dimension_semantics=("arbitrary",) * n_grid_axes,
        vmem_limit_bytes=V7X_VMEM_LIMIT_BYTES)


def _mm(a, b):
    return jnp.dot(a.astype(BF16), b.astype(BF16), preferred_element_type=F32)


def _mm_nt(a, b):
    return lax.dot_general(a.astype(BF16), b.astype(BF16), (((1,), (1,)), ((), ())),
                           preferred_element_type=F32)


def _sigmoid(x):
    return 1.0 / (1.0 + jnp.exp(-x))


def _layer_norm(y, g, b):
    mu = jnp.mean(y, axis=-1, keepdims=True)
    d = y - mu
    var = jnp.mean(d * d, axis=-1, keepdims=True)
    return d * lax.rsqrt(var + LN_EPS) * g + b


def _proj_kernel(x_ref, w_ref, *out_refs, segs, col_chunk):
    xb = x_ref[...].astype(BF16)
    for o_ref, (start, width) in zip(out_refs, segs):
        for c0 in range(0, width, col_chunk):
            c1 = min(c0 + col_chunk, width)
            acc = jnp.dot(xb, w_ref[:, start + c0:start + c1], preferred_element_type=F32)
            o_ref[:, c0:c1] = acc.astype(o_ref.dtype)


def _proj(x, w, segs, dtypes, tm):
    m, k = x.shape
    n = w.shape[1]
    assert m % tm == 0
    return pl.pallas_call(
        functools.partial(_proj_kernel, segs=segs, col_chunk=512),
        grid=(m // tm,),
        in_specs=[pl.BlockSpec((tm, k), lambda i: (i, 0)),
                  pl.BlockSpec((k, n), lambda i: (0, 0))],
        out_specs=[pl.BlockSpec((tm, wd), lambda i: (i, 0)) for _, wd in segs],
        out_shape=[jax.ShapeDtypeStruct((m, wd), dt) for (_, wd), dt in zip(segs, dtypes)],
        compiler_params=_cparams(1),
        name="proj",
    )(x, w)


def _rope_table_kernel(pos_ref, inv_ref, cos_ref, sin_ref):
    ang = pos_ref[...].astype(F32) * inv_ref[...]
    cos_ref[...] = jnp.cos(ang)
    sin_ref[...] = jnp.sin(ang)


def _rope_tables(positions):
    b, s = positions.shape
    half = SWA_DH // 2
    inv_freq = ROPE_THETA ** (-jnp.arange(0, SWA_DH, 2, dtype=F32) / SWA_DH)
    reps = V7X_LANES // half
    pos_rep = jnp.broadcast_to(positions[..., None], (b, s, half)).reshape(b * s // reps, V7X_LANES)
    inv_rep = jnp.tile(inv_freq, reps)[None, :]
    rows = pos_rep.shape[0]
    tr = min(rows, 2048)
    assert rows % tr == 0
    cos_d, sin_d = pl.pallas_call(
        _rope_table_kernel,
        grid=(rows // tr,),
        in_specs=[pl.BlockSpec((tr, V7X_LANES), lambda i: (i, 0)),
                  pl.BlockSpec((1, V7X_LANES), lambda i: (0, 0))],
        out_specs=[pl.BlockSpec((tr, V7X_LANES), lambda i: (i, 0))] * 2,
        out_shape=[jax.ShapeDtypeStruct((rows, V7X_LANES), F32)] * 2,
        compiler_params=_cparams(1),
        name="rope_tables",
    )(pos_rep, inv_rep)
    cos32 = cos_d.reshape(b, s, half)
    sin32 = sin_d.reshape(b, s, half)
    cos_t = jnp.tile(cos32, (1, 1, reps))
    sin_t = jnp.tile(jnp.concatenate([-sin32, sin32], axis=-1), (1, 1, reps // 2))
    return cos_t, sin_t


def _rope_pairs(x, cos_t, sin_t):
    lane = lax.broadcasted_iota(jnp.int32, x.shape, 1)
    first_half = (lane & (SWA_DH - 1)) < (SWA_DH // 2)
    swapped = jnp.where(first_half,
                        pltpu.roll(x, V7X_LANES - SWA_DH // 2, axis=1),
                        pltpu.roll(x, SWA_DH // 2, axis=1))
    return x * cos_t + swapped * sin_t


def _gdn_prep_kernel(x_ref, halo_ref, ab_ref, cw_ref, alog_ref, dtb_ref,
                     qkv_ref, beta_ref, gc_ref, gr_ref, xs_ref):
    ts = x_ref.shape[1]
    not_first = jnp.where(pl.program_id(1) > 0, 1.0, 0.0)
    xs_ref[0:CONV_HALO, :] = halo_ref[0].astype(F32) * not_first
    xs_ref[CONV_HALO:CONV_HALO + ts, :] = x_ref[0].astype(F32)

    n_blocks = (3 * DN_W) // V7X_LANES
    for blk in range(n_blocks):
        c0 = blk * V7X_LANES
        y = jnp.zeros((ts, V7X_LANES), F32)
        for j in range(CONV_WIDTH):
            r0 = CONV_HALO - (CONV_WIDTH - 1) + j
            y = y + cw_ref[j:j + 1, c0:c0 + V7X_LANES] * xs_ref[r0:r0 + ts, c0:c0 + V7X_LANES]
        y = y * _sigmoid(y)
        if blk < 2 * DN_HEADS:
            y = y * lax.rsqrt(jnp.sum(y * y, axis=-1, keepdims=True) + NORM_EPS)
        qkv_ref[0, :, c0:c0 + V7X_LANES] = y.astype(qkv_ref.dtype)

    ab = ab_ref[0]
    a = ab[:, :V7X_LANES]
    b = ab[:, V7X_LANES:]
    xg = a + dtb_ref[...]
    softplus = jnp.maximum(xg, 0.0) + jnp.log1p(jnp.exp(-jnp.abs(xg)))
    g = -jnp.exp(alog_ref[...]) * softplus
    beta_ref[0] = _sigmoid(b)

    row = lax.broadcasted_iota(jnp.int32, g.shape, 0) & (CHUNK - 1)
    gc = g
    step = 1
    while step < CHUNK:
        gc = gc + jnp.where(row >= step, pltpu.roll(gc, step, axis=0), 0.0)
        step *= 2
    gc_ref[0] = gc

    gct = jnp.transpose(gc)[0:8, :]
    plus = pltpu.roll(pltpu.roll(gct, CHUNK, axis=1), 7, axis=0)
    minus = pltpu.roll(pltpu.roll(gct, ts - CHUNK, axis=1), 1, axis=0)
    lane = lax.broadcasted_iota(jnp.int32, gct.shape, 1)
    low = (lane & (V7X_LANES - 1)) < CHUNK
    even = (lax.broadcasted_iota(jnp.int32, gct.shape, 0) & 1) == 0
    gr_ref[0] = jnp.where(even, jnp.where(low, gct, plus), jnp.where(low, minus, gct))


def _gdn_prep(qkv_pre, ab, conv_w, a_log, dt_bias, ts):
    b, s, w = qkv_pre.shape
    assert s % ts == 0 and ts % V7X_LANES == 0
    halo_blocks = ts // CONV_HALO
    pad = V7X_LANES - DN_HEADS
    alog = jnp.pad(a_log.astype(F32), (0, pad))[None, :]
    dtb = jnp.pad(dt_bias.astype(F32), (0, pad))[None, :]
    return pl.pallas_call(
        _gdn_prep_kernel,
        grid=(b, s // ts),
        in_specs=[
            pl.BlockSpec((1, ts, w), lambda bi, i: (bi, i, 0)),
            pl.BlockSpec((1, CONV_HALO, w), lambda bi, i: (bi, jnp.maximum(i * halo_blocks - 1, 0), 0)),
            pl.BlockSpec((1, ts, 2 * V7X_LANES), lambda bi, i: (bi, i, 0)),
            pl.BlockSpec((CONV_WIDTH, w), lambda bi, i: (0, 0)),
            pl.BlockSpec((1, V7X_LANES), lambda bi, i: (0, 0)),
            pl.BlockSpec((1, V7X_LANES), lambda bi, i: (0, 0)),
        ],
        out_specs=[
            pl.BlockSpec((1, ts, w), lambda bi, i: (bi, i, 0)),
            pl.BlockSpec((1, ts, V7X_LANES), lambda bi, i: (bi, i, 0)),
            pl.BlockSpec((1, ts, V7X_LANES), lambda bi, i: (bi, i, 0)),
            pl.BlockSpec((1, 8, ts), lambda bi, i: (bi, 0, i)),
        ],
        out_shape=[
            jax.ShapeDtypeStruct((b, s, w), BF16),
            jax.ShapeDtypeStruct((b, s, V7X_LANES), F32),
            jax.ShapeDtypeStruct((b, s, V7X_LANES), F32),
            jax.ShapeDtypeStruct((b, 8, s), F32),
        ],
        scratch_shapes=[pltpu.VMEM((CONV_HALO + ts, w), F32)],
        compiler_params=_cparams(2),
        name="gdn_prep",
    )(qkv_pre, qkv_pre, ab, conv_w.astype(F32), alog, dtb)


def _gdn_kernel(qkv_ref, z_ref, beta_ref, gc_ref, gr_ref, nw_ref, o_ref, state_ref):
    ts = qkv_ref.shape[1]
    c = CHUNK
    lanes = V7X_LANES

    @pl.when(pl.program_id(1) == 0)
    def _():
        state_ref[...] = jnp.zeros_like(state_ref)

    lane2 = lax.broadcasted_iota(jnp.int32, (c, 2 * lanes), 1)
    head_a2 = lane2 < lanes
    lane1 = lax.broadcasted_iota(jnp.int32, (c, lanes), 1)
    half_a = lane1 < c
    col = lane1 & (c - 1)
    row = lax.broadcasted_iota(jnp.int32, (c, lanes), 0)
    incl = row >= col
    strict = row > col
    eye = jnp.where(row == col, 1.0, 0.0)
    lane_sq = lax.broadcasted_iota(jnp.int32, (lanes, lanes), 1)
    half_a_sq = lane_sq < c
    nw = nw_ref[...]
    q_scale = DN_DK ** -0.5

    def split_halves(x, mask):
        return jnp.concatenate([jnp.where(mask, x, 0.0), jnp.where(mask, 0.0, x)], axis=0)

    items = [(p, ci) for ci in range(ts // c) for p in range(DN_PAIRS)]
    n_sq = c.bit_length() - 1

    xs, invs, intras, rhss, kdts, qes, cds = [], [], [], [], [], [], []
    for p, ci in items:
        ha, hb = 2 * p, 2 * p + 1
        r0 = ci * c
        qp = qkv_ref[0, r0:r0 + c, 2 * lanes * p:2 * lanes * (p + 1)].astype(F32)
        kp = qkv_ref[0, r0:r0 + c, DN_W + 2 * lanes * p:DN_W + 2 * lanes * (p + 1)].astype(F32)
        vp = qkv_ref[0, r0:r0 + c, 2 * DN_W + 2 * lanes * p:2 * DN_W + 2 * lanes * (p + 1)].astype(F32)
        beta_t = beta_ref[0, r0:r0 + c, :]
        gc_t = gc_ref[0, r0:r0 + c, :]
        b_a, b_b = beta_t[:, ha:ha + 1], beta_t[:, hb:hb + 1]
        g_a, g_b = gc_t[:, ha:ha + 1], gc_t[:, hb:hb + 1]
        beta2 = jnp.where(head_a2, b_a, b_b)
        g2 = jnp.where(head_a2, g_a, g_b)
        g1 = jnp.where(half_a, g_a, g_b)
        g_row = gr_ref[0, 2 * p + (ci % 2):2 * p + (ci % 2) + 1, lanes * (ci // 2):lanes * (ci // 2 + 1)]
        g_last = g2[c - 1:c, :]
        eg = jnp.exp(g2)
        kb = kp * beta2
        vb = vp * beta2
        kbe = kb * eg
        qs = qp * q_scale
        kd = kp * jnp.exp(g_last - g2)
        qes.append(qs * eg)
        cds.append(jnp.exp(g_last))

        r1 = _mm_nt(jnp.concatenate([kb, qs], axis=0), split_halves(kp, head_a2))
        decay = jnp.where(incl, jnp.exp(g1 - g_row), 0.0)
        x = jnp.where(strict, -(r1[:c] * decay), 0.0)
        xs.append(x)
        invs.append(eye + x)
        intras.append(r1[c:] * decay)
        rhss.append(jnp.concatenate([
            jnp.concatenate([vb[:, :lanes], kbe[:, :lanes]], axis=1),
            jnp.concatenate([vb[:, lanes:], kbe[:, lanes:]], axis=1)], axis=0).astype(BF16))
        kdts.append(jnp.transpose(jnp.concatenate([kd[:, :lanes], kd[:, lanes:]], axis=0)))

    xs = [_mm(x, split_halves(x, half_a)) for x in xs]
    for _ in range(n_sq - 2):
        rs = [_mm(jnp.concatenate([x, inv], axis=0), split_halves(x, half_a)) for x, inv in zip(xs, invs)]
        xs = [r[:c] for r in rs]
        invs = [inv + r[c:] for inv, r in zip(invs, rs)]
    invs = [inv + _mm(inv, split_halves(x, half_a)) for x, inv in zip(xs, invs)]

    sols = [_mm(split_halves(inv, half_a), rhs) for inv, rhs in zip(invs, rhss)]
    r3s = [_mm(jnp.concatenate([split_halves(intra, half_a), split_halves(kdt, half_a_sq)], axis=0), sol)
           for intra, kdt, sol in zip(intras, kdts, sols)]

    states = [state_ref[hd] for hd in range(DN_HEADS)]
    for (p, ci), r3, qe, cd in zip(items, r3s, qes, cds):
        r0 = ci * c
        for hx in range(2):
            hd = 2 * p + hx
            lo = hx * lanes
            s_x = states[hd]
            o0 = r3[hx * c:(hx + 1) * c, :lanes]
            iw = r3[hx * c:(hx + 1) * c, lanes:]
            n0 = r3[2 * c + hx * lanes:2 * c + (hx + 1) * lanes, :lanes]
            kw = r3[2 * c + hx * lanes:2 * c + (hx + 1) * lanes, lanes:]
            r4 = _mm(jnp.concatenate([qe[:, lo:lo + lanes] - iw, -kw], axis=0), s_x)
            o = r4[:c] + o0
            states[hd] = cd[:, lo:lo + lanes] * s_x + n0 + r4[c:]
            zt = z_ref[0, r0:r0 + c, hd * lanes:(hd + 1) * lanes].astype(F32)
            on = o * lax.rsqrt(jnp.mean(o * o, axis=-1, keepdims=True) + NORM_EPS) * nw
            o_ref[0, r0:r0 + c, hd * lanes:(hd + 1) * lanes] = (on * (zt * _sigmoid(zt))).astype(o_ref.dtype)
    for hd in range(DN_HEADS):
        state_ref[hd] = states[hd]


def _gdn(qkv, z, beta, gc, gr, norm_w, ts):
    b, s, w = qkv.shape
    return pl.pallas_call(
        _gdn_kernel,
        grid=(b, s // ts),
        in_specs=[
            pl.BlockSpec((1, ts, w), lambda bi, i: (bi, i, 0)),
            pl.BlockSpec((1, ts, DN_W), lambda bi, i: (bi, i, 0)),
            pl.BlockSpec((1, ts, V7X_LANES), lambda bi, i: (bi, i, 0)),
            pl.BlockSpec((1, ts, V7X_LANES), lambda bi, i: (bi, i, 0)),
            pl.BlockSpec((1, 8, ts), lambda bi, i: (bi, 0, i)),
            pl.BlockSpec((1, V7X_LANES), lambda bi, i: (0, 0)),
        ],
        out_specs=pl.BlockSpec((1, ts, DN_W), lambda bi, i: (bi, i, 0)),
        out_shape=jax.ShapeDtypeStruct((b, s, DN_W), BF16),
        scratch_shapes=[pltpu.VMEM((DN_HEADS, DN_DK, DN_DK), F32)],
        compiler_params=_cparams(2),
        name="gdn",
    )(qkv, z, beta, gc, gr, norm_w.astype(F32)[None, :])


def _swa_kernel(sink_ref, q_ref, kv_ref, kvh_ref, cos_ref, sin_ref, cosh_ref, sinh_ref, o_ref,
                kvar_ref, vvar_ref):
    tq = q_ref.shape[1]
    lanes = V7X_LANES
    w = WINDOW
    first_key = jnp.where(pl.program_id(1) == 0, w, 0)

    kv_all = jnp.concatenate([kvh_ref[0], kv_ref[0]], axis=0)
    cos_all = jnp.concatenate([cosh_ref[0], cos_ref[0]], axis=0)
    sin_all = jnp.concatenate([sinh_ref[0], sin_ref[0]], axis=0)
    k_rot = _rope_pairs(kv_all[:, :lanes], cos_all, sin_all)
    v_all = kv_all[:, lanes:]
    lane = lax.broadcasted_iota(jnp.int32, k_rot.shape, 1)
    low = lane < SWA_DH
    for src, dst in ((k_rot, kvar_ref), (v_all, vvar_ref)):
        swapped = pltpu.roll(src, SWA_DH, axis=1)
        dst[0] = jnp.where(low, src, 0.0).astype(BF16)
        dst[1] = jnp.where(low, 0.0, swapped).astype(BF16)
        dst[2] = jnp.where(low, swapped, 0.0).astype(BF16)
        dst[3] = jnp.where(low, 0.0, src).astype(BF16)

    scale = SWA_DH ** -0.5
    q_rot = []
    for p in range(SWA_PAIRS):
        qf = q_ref[0, :, p * lanes:(p + 1) * lanes].astype(F32)
        q_rot.append((_rope_pairs(qf, cos_ref[0], sin_ref[0]) * scale).astype(BF16))

    rows3 = SWA_PAIRS_PER_KV * w
    qi = lax.broadcasted_iota(jnp.int32, (rows3, 2 * w), 0) & (w - 1)
    kj = lax.broadcasted_iota(jnp.int32, (rows3, 2 * w), 1)
    dist = qi + w - kj
    band = (dist >= 0) & (dist < w)
    head_slot = lax.broadcasted_iota(jnp.int32, (rows3, 1), 0) // w
    neg_inf = jnp.float32(-jnp.inf)

    for j in range(tq // w):
        r0 = j * w
        if j == 0:
            valid = band & (kj >= first_key)
        else:
            valid = band
        for g in range(SWA_KV_HEADS):
            pairs = [SWA_PAIRS_PER_KV * g + t for t in range(SWA_PAIRS_PER_KV)]
            qs = jnp.concatenate([q_rot[p][r0:r0 + w] for p in pairs], axis=0)
            acc = jnp.zeros((rows3, lanes), F32)
            for e in range(2):
                kw_ = kvar_ref[2 * g + e, r0:r0 + 2 * w, :]
                vw_ = vvar_ref[2 * g + e, r0:r0 + 2 * w, :]
                s = lax.dot_general(qs, kw_, (((1,), (1,)), ((), ())), preferred_element_type=F32)
                s = jnp.where(valid, s, neg_inf)
                heads = [2 * p + e for p in pairs]
                sink = jnp.where(head_slot == 0, sink_ref[heads[0]],
                                 jnp.where(head_slot == 1, sink_ref[heads[1]], sink_ref[heads[2]]))
                m = jnp.maximum(jnp.max(s, axis=-1, keepdims=True), sink)
                pe = jnp.exp(s - m)
                den = jnp.sum(pe, axis=-1, keepdims=True) + jnp.exp(sink - m)
                probs = (pe * (1.0 / den)).astype(BF16)
                acc = acc + jnp.dot(probs, vw_, preferred_element_type=F32)
            for t, p in enumerate(pairs):
                o_ref[0, r0:r0 + w, p * lanes:(p + 1) * lanes] = acc[t * w:(t + 1) * w].astype(o_ref.dtype)


def _swa(q_pre, kv_sh, cos_t, sin_t, sinks, tq):
    b, s, _ = q_pre.shape
    assert s % tq == 0 and tq % WINDOW == 0
    hb = tq // WINDOW
    cur = lambda bi, i: (bi, i, 0)
    halo = lambda bi, i: (bi, jnp.maximum(i * hb - 1, 0), 0)
    lanes = V7X_LANES
    return pl.pallas_call(
        _swa_kernel,
        grid=(b, s // tq),
        in_specs=[
            pl.BlockSpec(memory_space=pltpu.SMEM),
            pl.BlockSpec((1, tq, SWA_Q_W), cur),
            pl.BlockSpec((1, tq, 2 * lanes), cur),
            pl.BlockSpec((1, WINDOW, 2 * lanes), halo),
            pl.BlockSpec((1, tq, lanes), cur),
            pl.BlockSpec((1, tq, lanes), cur),
            pl.BlockSpec((1, WINDOW, lanes), halo),
            pl.BlockSpec((1, WINDOW, lanes), halo),
        ],
        out_specs=pl.BlockSpec((1, tq, SWA_Q_W), cur),
        out_shape=jax.ShapeDtypeStruct((b, s, SWA_Q_W), BF16),
        scratch_shapes=[pltpu.VMEM((4, WINDOW + tq, lanes), BF16),
                        pltpu.VMEM((4, WINDOW + tq, lanes), BF16)],
        compiler_params=_cparams(2),
        name="swa",
    )(sinks.astype(F32), q_pre, kv_sh, kv_sh, cos_t, sin_t, cos_t, sin_t)


def _mix_out_kernel(o_ref, qm_ref, mk_ref, mv_ref, wo_ref, h_ref, g_ref, b_ref, out_ref):
    qm = qm_ref[0]
    mk = mk_ref[0].astype(F32)
    mv = mv_ref[0].astype(F32)
    lane_head = lax.broadcasted_iota(jnp.int32, mk.shape, 1) // MEM_DH
    scale = MEM_DH ** -0.5
    mo = jnp.zeros(qm.shape, F32)
    for hd in range(MEM_HEADS):
        kh = jnp.where(lane_head == hd, mk, 0.0).astype(BF16)
        vh = jnp.where(lane_head == hd, mv, 0.0).astype(BF16)
        s = lax.dot_general(qm, kh, (((1,), (1,)), ((), ())), preferred_element_type=F32) * scale
        m = jnp.max(s, axis=-1, keepdims=True)
        pe = jnp.exp(s - m)
        probs = pe * (1.0 / jnp.sum(pe, axis=-1, keepdims=True))
        mo = mo + jnp.dot(probs.astype(BF16), vh, preferred_element_type=F32)
    dn = o_ref.shape[2]
    mix = (jnp.dot(o_ref[0], wo_ref[0:dn, :], preferred_element_type=F32)
           + jnp.dot(mo.astype(BF16), wo_ref[dn:, :], preferred_element_type=F32))
    y = DN_ALPHA * h_ref[0] + mix
    out_ref[0] = _layer_norm(y, g_ref[...], b_ref[...])


def _mix_out(o, qm, mk, mv, wo, h, ln_g, ln_b, tm):
    b, s, d = h.shape
    m_tok = mk.shape[1]
    cur = lambda bi, i: (bi, i, 0)
    per_b = lambda bi, i: (bi, 0, 0)
    const = lambda bi, i: (0, 0)
    return pl.pallas_call(
        _mix_out_kernel,
        grid=(b, s // tm),
        in_specs=[
            pl.BlockSpec((1, tm, o.shape[2]), cur),
            pl.BlockSpec((1, tm, MEM_W), cur),
            pl.BlockSpec((1, m_tok, MEM_W), per_b),
            pl.BlockSpec((1, m_tok, MEM_W), per_b),
            pl.BlockSpec(wo.shape, const),
            pl.BlockSpec((1, tm, d), cur),
            pl.BlockSpec((1, d), const),
            pl.BlockSpec((1, d), const),
        ],
        out_specs=pl.BlockSpec((1, tm, d), cur),
        out_shape=jax.ShapeDtypeStruct((b, s, d), F32),
        compiler_params=_cparams(2),
        name="mix_out",
    )(o, qm, mk, mv, wo, h, ln_g[None, :], ln_b[None, :])


def _mlp_kernel(x_ref, up_ref, dn_ref, g_ref, b_ref, o_ref, *, hid_chunk):
    x = x_ref[...]
    xb = x.astype(BF16)
    hidden = up_ref.shape[1]
    acc = jnp.zeros(x.shape, F32)
    for j in range(0, hidden, hid_chunk):
        hid = jnp.dot(xb, up_ref[:, j:j + hid_chunk], preferred_element_type=F32)
        hid = jnp.square(jnp.maximum(hid, 0.0))
        acc = acc + jnp.dot(hid.astype(BF16), dn_ref[j:j + hid_chunk, :], preferred_element_type=F32)
    o_ref[...] = _layer_norm(DN_ALPHA * x + acc, g_ref[...], b_ref[...])


def _mlp(x, w_up, w_down, ln_g, ln_b, tm):
    m, d = x.shape
    hidden = w_up.shape[1]
    return pl.pallas_call(
        functools.partial(_mlp_kernel, hid_chunk=512),
        grid=(m // tm,),
        in_specs=[
            pl.BlockSpec((tm, d), lambda i: (i, 0)),
            pl.BlockSpec((d, hidden), lambda i: (0, 0), pipeline_mode=pl.Buffered(1)),
            pl.BlockSpec((hidden, d), lambda i: (0, 0), pipeline_mode=pl.Buffered(1)),
            pl.BlockSpec((1, d), lambda i: (0, 0)),
            pl.BlockSpec((1, d), lambda i: (0, 0)),
        ],
        out_specs=pl.BlockSpec((tm, d), lambda i: (i, 0)),
        out_shape=jax.ShapeDtypeStruct((m, d), F32),
        compiler_params=_cparams(1),
        name="mlp",
    )(x, w_up, w_down, ln_g[None, :], ln_b[None, :])


def _a_proj_weights(w_in):
    c1 = 3 * DN_W
    z_end = c1 + DN_W
    a_end = z_end + DN_HEADS
    b_end = a_end + DN_HEADS
    pad = V7X_LANES - DN_HEADS
    return jnp.concatenate([
        w_in[:, :z_end], w_in[:, b_end:],
        jnp.pad(w_in[:, z_end:a_end], ((0, 0), (0, pad))),
        jnp.pad(w_in[:, a_end:b_end], ((0, 0), (0, pad)))], axis=1).astype(BF16)


def _tile(n, pref):
    t = min(n, pref)
    assert n % t == 0
    return t


def kernel(x, mem, positions, a_w_in, a_conv_w, a_A_log, a_dt_bias, a_norm_w, b_w_in, b_sinks,
           w_kv_shared, mem_w_kv, w_o, mlp_w_up, mlp_w_down, ln_g, ln_b):
    b, s, d = x.shape
    t = b * s
    m_tok = mem.shape[1]
    tm = _tile(s, ROW_TILE)
    mem2 = mem.reshape(b * m_tok, d)
    h = x
    kv_sh = None
    cos_t = sin_t = None
    a_segs = ((0, 3 * DN_W), (3 * DN_W, DN_W), (4 * DN_W, MEM_W), (4 * DN_W + MEM_W, 2 * V7X_LANES))
    for layer in range(DEPTH):
        mk, mv = _proj(mem2, mem_w_kv[layer].astype(BF16), ((0, MEM_W), (MEM_W, MEM_W)), (BF16, BF16),
                       b * m_tok)
        mk = mk.reshape(b, m_tok, MEM_W)
        mv = mv.reshape(b, m_tok, MEM_W)
        h2 = h.reshape(t, d)
        if layer < N_A:
            qkv_pre, z, qm, ab = _proj(h2, _a_proj_weights(a_w_in[layer]), a_segs, (BF16, BF16, BF16, F32), tm)
            ts = _tile(s, GDN_TILE)
            qkv, beta, gc, gr = _gdn_prep(qkv_pre.reshape(b, s, -1), ab.reshape(b, s, -1), a_conv_w[layer],
                                          a_A_log[layer], a_dt_bias[layer], ts)
            o = _gdn(qkv, z.reshape(b, s, -1), beta, gc, gr, a_norm_w[layer], ts)
        else:
            j = layer - N_A
            q_pre, qm = _proj(h2, b_w_in[j].astype(BF16), ((0, SWA_Q_W), (SWA_Q_W, MEM_W)), (BF16, BF16), tm)
            o = _swa(q_pre.reshape(b, s, -1), kv_sh, cos_t, sin_t, b_sinks[j], _tile(s, SWA_TILE))
        h = _mix_out(o, qm.reshape(b, s, -1), mk, mv, w_o[layer].astype(BF16), h,
                     ln_g[layer, 0], ln_b[layer, 0], tm)
        h = _mlp(h.reshape(t, d), mlp_w_up[layer].astype(BF16), mlp_w_down[layer].astype(BF16),
                 ln_g[layer, 1], ln_b[layer, 1], tm).reshape(b, s, d)
        if layer == N_A - 1:
            (kv_pre,) = _proj(h.reshape(t, d), w_kv_shared.astype(BF16), ((0, 2 * V7X_LANES),), (F32,), tm)
            kv_sh = kv_pre.reshape(b, s, -1)
            cos_t, sin_t = _rope_tables(positions)
    return h
```

```python
import functools

import jax
import jax.numpy as jnp
from jax import lax
from jax.experimental import pallas as pl
from jax.experimental.pallas import tpu as pltpu

F32 = jnp.float32
BF16 = jnp.bfloat16

V7X_LANES = 128
V7X_VMEM_LIMIT_BYTES = 56 * 1024 * 1024
BF16_SUBLANES = 16

DEPTH = 4
N_A = DEPTH // 2
MEM_HEADS = 4
MEM_DH = 64
MEM_W = MEM_HEADS * MEM_DH
DN_DK = 128
DN_HEADS = 6
DN_PAIRS = DN_HEADS // 2
DN_W = DN_HEADS * DN_DK
CONV_WIDTH = 4
CONV_HALO = 16
CHUNK = 64
SWA_DH = 64
SWA_HEADS = 12
SWA_KV_HEADS = 2
SWA_Q_W = SWA_HEADS * SWA_DH
SWA_PAIRS = SWA_Q_W // V7X_LANES
SWA_PAIRS_PER_KV = SWA_PAIRS // SWA_KV_HEADS
WINDOW = 128
ROPE_THETA = 10000.0
LN_EPS = 1e-5
NORM_EPS = 1e-6
DN_ALPHA = (2.0 * DEPTH) ** 0.25
LOG2_E = 1.4426950408889634

ROW_TILE = 512
GDN_TILE = 256
SWA_TILE = 512


def _cparams(n_grid_axes):
    return pltpu.CompilerParams(
        dimension_semantics=("arbitrary",) * n_grid_axes,
        vmem_limit_bytes=V7X_VMEM_LIMIT_BYTES)


def _mm(a, b):
    return jnp.dot(a.astype(BF16), b.astype(BF16), preferred_element_type=F32)


def _mm_nt(a, b):
    return lax.dot_general(a.astype(BF16), b.astype(BF16), (((1,), (1,)), ((), ())),
                           preferred_element_type=F32)


def _sigmoid(x):
    return 1.0 / (1.0 + jnp.exp(-x))


def _layer_norm(y, g, b):
    mu = jnp.mean(y, axis=-1, keepdims=True)
    d = y - mu
    var = jnp.mean(d * d, axis=-1, keepdims=True)
    return d * lax.rsqrt(var + LN_EPS) * g + b


def _proj_kernel(x_ref, w_ref, *out_refs, segs, col_chunk):
    xb = x_ref[...].astype(BF16)
    for o_ref, (start, width) in zip(out_refs, segs):
        for c0 in range(0, width, col_chunk):
            c1 = min(c0 + col_chunk, width)
            acc = jnp.dot(xb, w_ref[:, start + c0:start + c1], preferred_element_type=F32)
            o_ref[:, c0:c1] = acc.astype(o_ref.dtype)


def _proj(x, w, segs, dtypes, tm):
    m, k = x.shape
    n = w.shape[1]
    assert m % tm == 0
    return pl.pallas_call(
        functools.partial(_proj_kernel, segs=segs, col_chunk=512),
        grid=(m // tm,),
        in_specs=[pl.BlockSpec((tm, k), lambda i: (i, 0)),
                  pl.BlockSpec((k, n), lambda i: (0, 0))],
        out_specs=[pl.BlockSpec((tm, wd), lambda i: (i, 0)) for _, wd in segs],
        out_shape=[jax.ShapeDtypeStruct((m, wd), dt) for (_, wd), dt in zip(segs, dtypes)],
        compiler_params=_cparams(1),
        name="proj",
    )(x, w)


def _rope_table_kernel(pos_ref, inv_ref, cos_ref, sin_ref):
    ang = pos_ref[...].astype(F32) * inv_ref[...]
    cos_ref[...] = jnp.cos(ang)
    sin_ref[...] = jnp.sin(ang)


def _rope_tables(positions):
    b, s = positions.shape
    half = SWA_DH // 2
    inv_freq = ROPE_THETA ** (-jnp.arange(0, SWA_DH, 2, dtype=F32) / SWA_DH)
    reps = V7X_LANES // half
    pos_rep = jnp.broadcast_to(positions[..., None], (b, s, half)).reshape(b * s // reps, V7X_LANES)
    inv_rep = jnp.tile(inv_freq, reps)[None, :]
    rows = pos_rep.shape[0]
    tr = min(rows, 2048)
    assert rows % tr == 0
    cos_d, sin_d = pl.pallas_call(
        _rope_table_kernel,
        grid=(rows // tr,),
        in_specs=[pl.BlockSpec((tr, V7X_LANES), lambda i: (i, 0)),
                  pl.BlockSpec((1, V7X_LANES), lambda i: (0, 0))],
        out_specs=[pl.BlockSpec((tr, V7X_LANES), lambda i: (i, 0))] * 2,
        out_shape=[jax.ShapeDtypeStruct((rows, V7X_LANES), F32)] * 2,
        compiler_params=_cparams(1),
        name="rope_tables",
    )(pos_rep, inv_rep)
    cos32 = cos_d.reshape(b, s, half)
    sin32 = sin_d.reshape(b, s, half)
    cos_t = jnp.tile(cos32, (1, 1, reps))
    sin_t = jnp.tile(jnp.concatenate([-sin32, sin32], axis=-1), (1, 1, reps // 2))
    return cos_t, sin_t


def _rope_pairs(x, cos_t, sin_t):
    lane = lax.broadcasted_iota(jnp.int32, x.shape, 1)
    first_half = (lane & (SWA_DH - 1)) < (SWA_DH // 2)
    swapped = jnp.where(first_half,
                        pltpu.roll(x, V7X_LANES - SWA_DH // 2, axis=1),
                        pltpu.roll(x, SWA_DH // 2, axis=1))
    return x * cos_t + swapped * sin_t


def _gdn_prep_kernel(x_ref, halo_ref, ab_ref, cw_ref, alog_ref, dtb_ref,
                     qkv_ref, beta_ref, gc_ref, gr_ref, xs_ref):
    ts = x_ref.shape[1]
    not_first = jnp.where(pl.program_id(1) > 0, 1.0, 0.0)
    xs_ref[0:CONV_HALO, :] = halo_ref[0].astype(F32) * not_first
    xs_ref[CONV_HALO:CONV_HALO + ts, :] = x_ref[0].astype(F32)

    n_blocks = (3 * DN_W) // V7X_LANES
    for blk in range(n_blocks):
        c0 = blk * V7X_LANES
        y = cw_ref[CONV_WIDTH - 1:CONV_WIDTH, c0:c0 + V7X_LANES] * xs_ref[CONV_HALO:CONV_HALO + ts, c0:c0 + V7X_LANES]
        for j in range(CONV_WIDTH - 1):
            r0 = CONV_HALO - (CONV_WIDTH - 1) + j
            y = y + cw_ref[j:j + 1, c0:c0 + V7X_LANES] * xs_ref[r0:r0 + ts, c0:c0 + V7X_LANES]
        y = y * _sigmoid(y)
        if blk < 2 * DN_HEADS:
            y = y * lax.rsqrt(jnp.sum(y * y, axis=-1, keepdims=True) + NORM_EPS)
        qkv_ref[0, :, c0:c0 + V7X_LANES] = y.astype(qkv_ref.dtype)

    ab = ab_ref[0]
    a = ab[:, :V7X_LANES]
    b = ab[:, V7X_LANES:]
    xg = a + dtb_ref[...]
    softplus = jnp.maximum(xg, 0.0) + jnp.log1p(jnp.exp(-jnp.abs(xg)))
    g = -jnp.exp(alog_ref[...]) * softplus
    beta_ref[0] = _sigmoid(b)

    row = lax.broadcasted_iota(jnp.int32, g.shape, 0) & (CHUNK - 1)
    gc = g
    step = 1
    while step < CHUNK:
        gc = gc + jnp.where(row >= step, pltpu.roll(gc, step, axis=0), 0.0)
        step *= 2
    gc_ref[0] = gc

    gct = jnp.transpose(gc)[0:8, :]
    plus = pltpu.roll(pltpu.roll(gct, CHUNK, axis=1), 7, axis=0)
    minus = pltpu.roll(pltpu.roll(gct, ts - CHUNK, axis=1), 1, axis=0)
    lane = lax.broadcasted_iota(jnp.int32, gct.shape, 1)
    low = (lane & (V7X_LANES - 1)) < CHUNK
    even = (lax.broadcasted_iota(jnp.int32, gct.shape, 0) & 1) == 0
    gr_ref[0] = jnp.where(even, jnp.where(low, gct, plus), jnp.where(low, minus, gct))


def _gdn_prep(qkv_pre, ab, conv_w, a_log, dt_bias, ts):
    b, s, w = qkv_pre.shape
    assert s % ts == 0 and ts % V7X_LANES == 0
    halo_blocks = ts // CONV_HALO
    pad = V7X_LANES - DN_HEADS
    alog = jnp.pad(a_log.astype(F32), (0, pad))[None, :]
    dtb = jnp.pad(dt_bias.astype(F32), (0, pad))[None, :]
    return pl.pallas_call(
        _gdn_prep_kernel,
        grid=(b, s // ts),
        in_specs=[
            pl.BlockSpec((1, ts, w), lambda bi, i: (bi, i, 0)),
            pl.BlockSpec((1, CONV_HALO, w), lambda bi, i: (bi, jnp.maximum(i * halo_blocks - 1, 0), 0)),
            pl.BlockSpec((1, ts, 2 * V7X_LANES), lambda bi, i: (bi, i, 0)),
            pl.BlockSpec((CONV_WIDTH, w), lambda bi, i: (0, 0)),
            pl.BlockSpec((1, V7X_LANES), lambda bi, i: (0, 0)),
            pl.BlockSpec((1, V7X_LANES), lambda bi, i: (0, 0)),
        ],
        out_specs=[
            pl.BlockSpec((1, ts, w), lambda bi, i: (bi, i, 0)),
            pl.BlockSpec((1, ts, V7X_LANES), lambda bi, i: (bi, i, 0)),
            pl.BlockSpec((1, ts, V7X_LANES), lambda bi, i: (bi, i, 0)),
            pl.BlockSpec((1, 8, ts), lambda bi, i: (bi, 0, i)),
        ],
        out_shape=[
            jax.ShapeDtypeStruct((b, s, w), BF16),
            jax.ShapeDtypeStruct((b, s, V7X_LANES), F32),
            jax.ShapeDtypeStruct((b, s, V7X_LANES), F32),
            jax.ShapeDtypeStruct((b, 8, s), F32),
        ],
        scratch_shapes=[pltpu.VMEM((CONV_HALO + ts, w), F32)],
        compiler_params=_cparams(2),
        name="gdn_prep",
    )(qkv_pre, qkv_pre, ab, conv_w.astype(F32), alog, dtb)


def _gdn_kernel(qkv_ref, z_ref, beta_ref, gc_ref, gr_ref, nw_ref, o_ref, state_ref):
    ts = qkv_ref.shape[1]
    c = CHUNK
    lanes = V7X_LANES

    @pl.when(pl.program_id(1) == 0)
    def _():
        state_ref[...] = jnp.zeros_like(state_ref)

    lane2 = lax.broadcasted_iota(jnp.int32, (c, 2 * lanes), 1)
    head_a2 = lane2 < lanes
    lane1 = lax.broadcasted_iota(jnp.int32, (c, lanes), 1)
    half_a = lane1 < c
    col = lane1 & (c - 1)
    row = lax.broadcasted_iota(jnp.int32, (c, lanes), 0)
    incl = row >= col
    strict = row > col
    eye = jnp.where(row == col, 1.0, 0.0)
    lane_sq = lax.broadcasted_iota(jnp.int32, (lanes, lanes), 1)
    half_a_sq = lane_sq < c
    nw = nw_ref[...]
    q_scale = DN_DK ** -0.5

    def split_halves(x, mask):
        return jnp.concatenate([jnp.where(mask, x, 0.0), jnp.where(mask, 0.0, x)], axis=0)

    items = [(p, ci) for ci in range(ts // c) for p in range(DN_PAIRS)]
    n_sq = c.bit_length() - 1

    xs, invs, intras, rhss, kdts, qes, cds = [], [], [], [], [], [], []
    for p, ci in items:
        ha, hb = 2 * p, 2 * p + 1
        r0 = ci * c
        qp = qkv_ref[0, r0:r0 + c, 2 * lanes * p:2 * lanes * (p + 1)].astype(F32)
        kp = qkv_ref[0, r0:r0 + c, DN_W + 2 * lanes * p:DN_W + 2 * lanes * (p + 1)].astype(F32)
        vp = qkv_ref[0, r0:r0 + c, 2 * DN_W + 2 * lanes * p:2 * DN_W + 2 * lanes * (p + 1)].astype(F32)
        beta_t = beta_ref[0, r0:r0 + c, :]
        gc_t = gc_ref[0, r0:r0 + c, :]
        b_a, b_b = beta_t[:, ha:ha + 1], beta_t[:, hb:hb + 1]
        g_a, g_b = gc_t[:, ha:ha + 1], gc_t[:, hb:hb + 1]
        beta2 = jnp.where(head_a2, b_a, b_b)
        g2 = jnp.where(head_a2, g_a, g_b)
        g1 = jnp.where(half_a, g_a, g_b)
        g_row = gr_ref[0, 2 * p + (ci % 2):2 * p + (ci % 2) + 1, lanes * (ci // 2):lanes * (ci // 2 + 1)]
        g_last = g2[c - 1:c, :]
        eg = jnp.exp(g2)
        kb = kp * beta2
        vb = vp * beta2
        kbe = kb * eg
        qs = qp * q_scale
        kd = kp * jnp.exp(g_last - g2)
        qes.append(qs * eg)
        cds.append(jnp.exp(g_last))

        r1 = _mm_nt(jnp.concatenate([kb, qs], axis=0), split_halves(kp, head_a2))
        decay = jnp.where(incl, jnp.exp(g1 - g_row), 0.0)
        x = jnp.where(strict, -(r1[:c] * decay), 0.0)
        xs.append(x)
        invs.append(eye + x)
        intras.append(r1[c:] * decay)
        rhss.append(jnp.concatenate([
            jnp.concatenate([vb[:, :lanes], kbe[:, :lanes]], axis=1),
            jnp.concatenate([vb[:, lanes:], kbe[:, lanes:]], axis=1)], axis=0).astype(BF16))
        kdts.append(jnp.transpose(jnp.concatenate([kd[:, :lanes], kd[:, lanes:]], axis=0)))

    xs = [_mm(x, split_halves(x, half_a)) for x in xs]
    for _ in range(n_sq - 2):
        rs = [_mm(jnp.concatenate([x, inv], axis=0), split_halves(x, half_a)) for x, inv in zip(xs, invs)]
        xs = [r[:c] for r in rs]
        invs = [inv + r[c:] for inv, r in zip(invs, rs)]
    invs = [inv + _mm(inv, split_halves(x, half_a)) for x, inv in zip(xs, invs)]

    sols = [_mm(split_halves(inv, half_a), rhs) for inv, rhs in zip(invs, rhss)]
    r3s = [_mm(jnp.concatenate([split_halves(intra, half_a), split_halves(kdt, half_a_sq)], axis=0), sol)
           for intra, kdt, sol in zip(intras, kdts, sols)]

    states = [state_ref[hd] for hd in range(DN_HEADS)]
    for (p, ci), r3, qe, cd in zip(items, r3s, qes, cds):
        r0 = ci * c
        for hx in range(2):
            hd = 2 * p + hx
            lo = hx * lanes
            s_x = states[hd]
            o0 = r3[hx * c:(hx + 1) * c, :lanes]
            iw = r3[hx * c:(hx + 1) * c, lanes:]
            n0 = r3[2 * c + hx * lanes:2 * c + (hx + 1) * lanes, :lanes]
            kw = r3[2 * c + hx * lanes:2 * c + (hx + 1) * lanes, lanes:]
            r4 = _mm(jnp.concatenate([qe[:, lo:lo + lanes] - iw, -kw], axis=0), s_x)
            o = r4[:c] + o0
            states[hd] = cd[:, lo:lo + lanes] * s_x + n0 + r4[c:]
            zt = z_ref[0, r0:r0 + c, hd * lanes:(hd + 1) * lanes].astype(F32)
            on = o * lax.rsqrt(jnp.mean(o * o, axis=-1, keepdims=True) + NORM_EPS) * nw
            o_ref[0, r0:r0 + c, hd * lanes:(hd + 1) * lanes] = (on * (zt * _sigmoid(zt))).astype(o_ref.dtype)
    for hd in range(DN_HEADS):
        state_ref[hd] = states[hd]


def _gdn(qkv, z, beta, gc, gr, norm_w, ts):
    b, s, w = qkv.shape
    return pl.pallas_call(
        _gdn_kernel,
        grid=(b, s // ts),
        in_specs=[
            pl.BlockSpec((1, ts, w), lambda bi, i: (bi, i, 0)),
            pl.BlockSpec((1, ts, DN_W), lambda bi, i: (bi, i, 0)),
            pl.BlockSpec((1, ts, V7X_LANES), lambda bi, i: (bi, i, 0)),
            pl.BlockSpec((1, ts, V7X_LANES), lambda bi, i: (bi, i, 0)),
            pl.BlockSpec((1, 8, ts), lambda bi, i: (bi, 0, i)),
            pl.BlockSpec((1, V7X_LANES), lambda bi, i: (0, 0)),
        ],
        out_specs=pl.BlockSpec((1, ts, DN_W), lambda bi, i: (bi, i, 0)),
        out_shape=jax.ShapeDtypeStruct((b, s, DN_W), BF16),
        scratch_shapes=[pltpu.VMEM((DN_HEADS, DN_DK, DN_DK), F32)],
        compiler_params=_cparams(2),
        name="gdn",
    )(qkv, z, beta, gc, gr, norm_w.astype(F32)[None, :])


def _swa_kernel(sink_ref, q_ref, kv_ref, kvh_ref, cos_ref, sin_ref, cosh_ref, sinh_ref, o_ref,
                kvar_ref, vt_ref):
    tq = q_ref.shape[1]
    lanes = V7X_LANES
    w = WINDOW
    first_key = jnp.where(pl.program_id(1) == 0, w, 0)

    kv_all = jnp.concatenate([kvh_ref[0], kv_ref[0]], axis=0)
    cos_all = jnp.concatenate([cosh_ref[0], cos_ref[0]], axis=0)
    sin_all = jnp.concatenate([sinh_ref[0], sin_ref[0]], axis=0)
    k_rot = _rope_pairs(kv_all[:, :lanes], cos_all, sin_all)
    v_all = kv_all[:, lanes:]
    lane = lax.broadcasted_iota(jnp.int32, k_rot.shape, 1)
    low = lane < SWA_DH
    k_sw = pltpu.roll(k_rot, SWA_DH, axis=1)
    kvar_ref[0] = jnp.where(low, k_rot, 0.0).astype(BF16)
    kvar_ref[1] = jnp.where(low, 0.0, k_sw).astype(BF16)
    kvar_ref[2] = jnp.where(low, k_sw, 0.0).astype(BF16)
    kvar_ref[3] = jnp.where(low, 0.0, k_rot).astype(BF16)
    v_sw = pltpu.roll(v_all, SWA_DH, axis=1)
    ones_hi = jnp.where(lane == SWA_DH, 1.0, 0.0)
    ones_lo = jnp.where(lane == 0, 1.0, 0.0)
    vt_ref[0] = jnp.transpose(jnp.where(low, v_all, ones_hi)).astype(BF16)
    vt_ref[1] = jnp.transpose(jnp.where(low, ones_lo, v_sw)).astype(BF16)
    vt_ref[2] = jnp.transpose(jnp.where(low, v_sw, ones_hi)).astype(BF16)
    vt_ref[3] = jnp.transpose(jnp.where(low, ones_lo, v_all)).astype(BF16)
    sum_row = (SWA_DH, 0)

    q_scale = SWA_DH ** -0.5 * LOG2_E
    q_t = []
    for p in range(SWA_PAIRS):
        qf = q_ref[0, :, p * lanes:(p + 1) * lanes].astype(F32)
        q_t.append(jnp.transpose(_rope_pairs(qf, cos_ref[0], sin_ref[0]) * q_scale).astype(BF16))

    cols3 = SWA_PAIRS_PER_KV * w
    kj = lax.broadcasted_iota(jnp.int32, (2 * w, cols3), 0)
    qi = lax.broadcasted_iota(jnp.int32, (2 * w, cols3), 1) & (w - 1)
    dist = qi + w - kj
    band = (dist >= 0) & (dist < w)
    neg_inf = jnp.float32(-jnp.inf)
    bias_band = jnp.where(band, 0.0, neg_inf)
    bias_first = jnp.where(band & (kj >= first_key), 0.0, neg_inf)
    head_slot = lax.broadcasted_iota(jnp.int32, (1, cols3), 1) // w
    row_low = lax.broadcasted_iota(jnp.int32, (lanes, cols3), 0) < SWA_DH
    combos = [(g, e) for g in range(SWA_KV_HEADS) for e in range(2)]
    group_pairs = [[SWA_PAIRS_PER_KV * g + t for t in range(SWA_PAIRS_PER_KV)] for g in range(SWA_KV_HEADS)]

    def scores(j):
        r0 = j * w
        out = []
        for g, e in combos:
            qs = jnp.concatenate([q_t[p][:, r0:r0 + w] for p in group_pairs[g]], axis=1)
            out.append(jnp.dot(kvar_ref[2 * g + e, r0:r0 + 2 * w, :], qs, preferred_element_type=F32))
        return out

    def finish(j, ss):
        r0 = j * w
        bias = bias_first if j == 0 else bias_band
        pes, sink_terms = [], []
        for (g, e), s in zip(combos, ss):
            s = s + bias
            hs = [2 * p + e for p in group_pairs[g]]
            sink = jnp.where(head_slot == 0, sink_ref[hs[0]],
                             jnp.where(head_slot == 1, sink_ref[hs[1]], sink_ref[hs[2]])) * LOG2_E
            m = jnp.maximum(jnp.max(s, axis=0, keepdims=True), sink)
            pes.append(jnp.exp2(s - m).astype(BF16))
            sink_terms.append(jnp.exp2(sink - m))
        pvs = [jnp.dot(vt_ref[2 * g + e, :, r0:r0 + 2 * w], pe, preferred_element_type=F32)
               for (g, e), pe in zip(combos, pes)]
        for g in range(SWA_KV_HEADS):
            normed = []
            for e in range(2):
                pv = pvs[2 * g + e]
                den = pv[sum_row[e]:sum_row[e] + 1, :] + sink_terms[2 * g + e]
                normed.append(pv * (1.0 / den))
            acc = jnp.where(row_low, normed[0], normed[1])
            for t, p in enumerate(group_pairs[g]):
                o_ref[0, r0:r0 + w, p * lanes:(p + 1) * lanes] = (
                    jnp.transpose(acc[:, t * w:(t + 1) * w]).astype(o_ref.dtype))

    n_blocks = tq // w
    ss = scores(0)
    for j in range(n_blocks):
        nxt = scores(j + 1) if j + 1 < n_blocks else None
        finish(j, ss)
        ss = nxt


def _swa(q_pre, kv_sh, cos_t, sin_t, sinks, tq):
    b, s, _ = q_pre.shape
    assert s % tq == 0 and tq % WINDOW == 0
    hb = tq // WINDOW
    cur = lambda bi, i: (bi, i, 0)
    halo = lambda bi, i: (bi, jnp.maximum(i * hb - 1, 0), 0)
    lanes = V7X_LANES
    return pl.pallas_call(
        _swa_kernel,
        grid=(b, s // tq),
        in_specs=[
            pl.BlockSpec(memory_space=pltpu.SMEM),
            pl.BlockSpec((1, tq, SWA_Q_W), cur),
            pl.BlockSpec((1, tq, 2 * lanes), cur),
            pl.BlockSpec((1, WINDOW, 2 * lanes), halo),
            pl.BlockSpec((1, tq, lanes), cur),
            pl.BlockSpec((1, tq, lanes), cur),
            pl.BlockSpec((1, WINDOW, lanes), halo),
            pl.BlockSpec((1, WINDOW, lanes), halo),
        ],
        out_specs=pl.BlockSpec((1, tq, SWA_Q_W), cur),
        out_shape=jax.ShapeDtypeStruct((b, s, SWA_Q_W), BF16),
        scratch_shapes=[pltpu.VMEM((4, WINDOW + tq, lanes), BF16),
                        pltpu.VMEM((4, lanes, WINDOW + tq), BF16)],
        compiler_params=_cparams(2),
        name="swa",
    )(sinks.astype(F32), q_pre, kv_sh, kv_sh, cos_t, sin_t, cos_t, sin_t)


def _mix_out_kernel(o_ref, qm_ref, mk_ref, mvt_ref, wo_ref, h_ref, g_ref, b_ref, out_ref):
    mk = mk_ref[0].astype(F32)
    mvt = mvt_ref[0]
    lane_head = lax.broadcasted_iota(jnp.int32, mk.shape, 1) // MEM_DH
    heads = range(MEM_HEADS)
    khs = [jnp.where(lane_head == hd, mk, 0.0).astype(BF16) for hd in heads]
    ones = jnp.ones((BF16_SUBLANES, mvt.shape[1]), BF16)
    vhs = [jnp.concatenate([mvt[hd * MEM_DH:(hd + 1) * MEM_DH, :], ones], axis=0) for hd in heads]
    dn = o_ref.shape[2]
    qm = qm_ref[0].astype(F32) * (MEM_DH ** -0.5 * LOG2_E)
    qm_t = jnp.transpose(qm).astype(BF16)
    ss = [jnp.dot(kh, qm_t, preferred_element_type=F32) for kh in khs]
    mix = jnp.dot(o_ref[0], wo_ref[0:dn, :], preferred_element_type=F32)
    pes = [jnp.exp2(s - jnp.max(s, axis=0, keepdims=True)).astype(BF16) for s in ss]
    parts = []
    for vh, pe in zip(vhs, pes):
        pv = jnp.dot(vh, pe, preferred_element_type=F32)
        parts.append(pv[:MEM_DH] * (1.0 / pv[MEM_DH:MEM_DH + 1]))
    mo = jnp.transpose(jnp.concatenate(parts, axis=0))
    mix = mix + jnp.dot(mo.astype(BF16), wo_ref[dn:, :], preferred_element_type=F32)
    out_ref[0] = _layer_norm(DN_ALPHA * h_ref[0] + mix, g_ref[...], b_ref[...])


def _mix_out(o, qm, mk, mvt, wo, h, ln_g, ln_b, tm):
    b, s, d = h.shape
    m_tok = mk.shape[1]
    cur = lambda bi, i: (bi, i, 0)
    per_b = lambda bi, i: (bi, 0, 0)
    const = lambda bi, i: (0, 0)
    return pl.pallas_call(
        _mix_out_kernel,
        grid=(b, s // tm),
        in_specs=[
            pl.BlockSpec((1, tm, o.shape[2]), cur),
            pl.BlockSpec((1, tm, MEM_W), cur),
            pl.BlockSpec((1, m_tok, MEM_W), per_b),
            pl.BlockSpec((1, MEM_W, m_tok), per_b),
            pl.BlockSpec(wo.shape, const),
            pl.BlockSpec((1, tm, d), cur),
            pl.BlockSpec((1, d), const),
            pl.BlockSpec((1, d), const),
        ],
        out_specs=pl.BlockSpec((1, tm, d), cur),
        out_shape=jax.ShapeDtypeStruct((b, s, d), F32),
        compiler_params=_cparams(2),
        name="mix_out",
    )(o, qm, mk, mvt, wo, h, ln_g[None, :], ln_b[None, :])


def _mlp_kernel(x_ref, up_ref, dn_ref, g_ref, b_ref, o_ref, *, hid_chunk):
    x = x_ref[...]
    xb = x.astype(BF16)
    hidden = up_ref.shape[1]
    acc = jnp.zeros(x.shape, F32)
    for j in range(0, hidden, hid_chunk):
        hid = jnp.dot(xb, up_ref[:, j:j + hid_chunk], preferred_element_type=F32)
        hid = jnp.square(jnp.maximum(hid, 0.0))
        acc = acc + jnp.dot(hid.astype(BF16), dn_ref[j:j + hid_chunk, :], preferred_element_type=F32)
    o_ref[...] = _layer_norm(DN_ALPHA * x + acc, g_ref[...], b_ref[...])


def _mlp(x, w_up, w_down, ln_g, ln_b, tm):
    m, d = x.shape
    hidden = w_up.shape[1]
    return pl.pallas_call(
        functools.partial(_mlp_kernel, hid_chunk=512),
        grid=(m // tm,),
        in_specs=[
            pl.BlockSpec((tm, d), lambda i: (i, 0)),
            pl.BlockSpec((d, hidden), lambda i: (0, 0), pipeline_mode=pl.Buffered(1)),
            pl.BlockSpec((hidden, d), lambda i: (0, 0), pipeline_mode=pl.Buffered(1)),
            pl.BlockSpec((1, d), lambda i: (0, 0)),
            pl.BlockSpec((1, d), lambda i: (0, 0)),
        ],
        out_specs=pl.BlockSpec((tm, d), lambda i: (i, 0)),
        out_shape=jax.ShapeDtypeStruct((m, d), F32),
        compiler_params=_cparams(1),
        name="mlp",
    )(x, w_up, w_down, ln_g[None, :], ln_b[None, :])


A_TAIL = 3 * V7X_LANES


def _a_weights_kernel(main_ref, tail_ref, o_ref):
    lanes = V7X_LANES
    z_end = 4 * DN_W
    o_ref[:, 0:z_end] = main_ref[...].astype(BF16)
    tail = tail_ref[...]
    lane = lax.broadcasted_iota(jnp.int32, (tail.shape[0], lanes), 1)
    qm = pltpu.roll(tail, A_TAIL - 2 * DN_HEADS, axis=1)[:, 0:MEM_W]
    first = tail[:, 0:lanes]
    a = jnp.where(lane < DN_HEADS, first, 0.0)
    b = jnp.where(lane < DN_HEADS, pltpu.roll(first, lanes - DN_HEADS, axis=1), 0.0)
    o_ref[:, z_end:z_end + MEM_W] = qm.astype(BF16)
    o_ref[:, z_end + MEM_W:z_end + MEM_W + lanes] = a.astype(BF16)
    o_ref[:, z_end + MEM_W + lanes:] = b.astype(BF16)


def _a_proj_weights(w_in_all, layer):
    _, k, n = w_in_all.shape
    z_end = 4 * DN_W
    assert n == z_end + 2 * DN_HEADS + MEM_W and z_end % A_TAIL == 0
    tk = 256
    return pl.pallas_call(
        _a_weights_kernel,
        grid=(k // tk,),
        in_specs=[pl.BlockSpec((None, tk, z_end), lambda i: (layer, i, 0)),
                  pl.BlockSpec((None, tk, A_TAIL), lambda i: (layer, i, z_end // A_TAIL))],
        out_specs=pl.BlockSpec((tk, z_end + MEM_W + 2 * V7X_LANES), lambda i: (i, 0)),
        out_shape=jax.ShapeDtypeStruct((k, z_end + MEM_W + 2 * V7X_LANES), BF16),
        compiler_params=_cparams(1),
        name="a_weights",
    )(w_in_all, w_in_all)


def _tile(n, pref):
    t = min(n, pref)
    assert n % t == 0
    return t


def kernel(x, mem, positions, a_w_in, a_conv_w, a_A_log, a_dt_bias, a_norm_w, b_w_in, b_sinks,
           w_kv_shared, mem_w_kv, w_o, mlp_w_up, mlp_w_down, ln_g, ln_b):
    b, s, d = x.shape
    t = b * s
    m_tok = mem.shape[1]
    tm = _tile(s, ROW_TILE)
    mem2 = mem.reshape(b * m_tok, d)
    h = x
    kv_sh = None
    cos_t = sin_t = None
    a_segs = ((0, 3 * DN_W), (3 * DN_W, DN_W), (4 * DN_W, MEM_W), (4 * DN_W + MEM_W, 2 * V7X_LANES))
    for layer in range(DEPTH):
        mk, mv = _proj(mem2, mem_w_kv[layer].astype(BF16), ((0, MEM_W), (MEM_W, MEM_W)), (BF16, BF16),
                       b * m_tok)
        mk = mk.reshape(b, m_tok, MEM_W)
        mvt = jnp.swapaxes(mv.reshape(b, m_tok, MEM_W), 1, 2)
        h2 = h.reshape(t, d)
        if layer < N_A:
            qkv_pre, z, qm, ab = _proj(h2, _a_proj_weights(a_w_in, layer), a_segs, (BF16, BF16, BF16, F32), tm)
            ts = _tile(s, GDN_TILE)
            qkv, beta, gc, gr = _gdn_prep(qkv_pre.reshape(b, s, -1), ab.reshape(b, s, -1), a_conv_w[layer],
                                          a_A_log[layer], a_dt_bias[layer], ts)
            o = _gdn(qkv, z.reshape(b, s, -1), beta, gc, gr, a_norm_w[layer], ts)
        else:
            j = layer - N_A
            q_pre, qm = _proj(h2, b_w_in[j].astype(BF16), ((0, SWA_Q_W), (SWA_Q_W, MEM_W)), (BF16, BF16), tm)
            o = _swa(q_pre.reshape(b, s, -1), kv_sh, cos_t, sin_t, b_sinks[j], _tile(s, SWA_TILE))
        h = _mix_out(o, qm.reshape(b, s, -1), mk, mvt, w_o[layer].astype(BF16), h,
                     ln_g[layer, 0], ln_b[layer, 0], tm)
        h = _mlp(h.reshape(t, d), mlp_w_up[layer].astype(BF16), mlp_w_down[layer].astype(BF16),
                 ln_g[layer, 1], ln_b[layer, 1], tm).reshape(b, s, d)
        if layer == N_A - 1:
            (kv_pre,) = _proj(h.reshape(t, d), w_kv_shared.astype(BF16), ((0, 2 * V7X_LANES),), (F32,), tm)
            kv_sh = kv_pre.reshape(b, s, -1)
            cos_t, sin_t = _rope_tables(positions)
    return h
```

```python
import functools

import jax
import jax.numpy as jnp
from jax import lax
from jax.experimental import pallas as pl
from jax.experimental.pallas import tpu as pltpu

F32 = jnp.float32
BF16 = jnp.bfloat16

V7X_LANES = 128
V7X_VMEM_LIMIT_BYTES = 56 * 1024 * 1024
BF16_SUBLANES = 16

DEPTH = 4
N_A = DEPTH // 2
MEM_HEADS = 4
MEM_DH = 64
MEM_W = MEM_HEADS * MEM_DH
DN_DK = 128
DN_HEADS = 6
DN_PAIRS = DN_HEADS // 2
DN_W = DN_HEADS * DN_DK
CONV_WIDTH = 4
CHUNK = 64
SWA_DH = 64
SWA_HEADS = 12
SWA_KV_HEADS = 2
SWA_Q_W = SWA_HEADS * SWA_DH
SWA_PAIRS = SWA_Q_W // V7X_LANES
SWA_PAIRS_PER_KV = SWA_PAIRS // SWA_KV_HEADS
WINDOW = 128
ROPE_THETA = 10000.0
LN_EPS = 1e-5
NORM_EPS = 1e-6
DN_ALPHA = (2.0 * DEPTH) ** 0.25
LOG2_E = 1.4426950408889634

ROW_TILE = 1024
GDN_TILE = 256
SWA_TILE = 512


def _cparams(n_grid_axes):
    return pltpu.CompilerParams(
        dimension_semantics=("arbitrary",) * n_grid_axes,
        vmem_limit_bytes=V7X_VMEM_LIMIT_BYTES)


def _mm(a, b):
    return jnp.dot(a.astype(BF16), b.astype(BF16), preferred_element_type=F32)


def _mm_nt(a, b):
    return lax.dot_general(a.astype(BF16), b.astype(BF16), (((1,), (1,)), ((), ())),
                           preferred_element_type=F32)


def _sigmoid(x):
    return 1.0 / (1.0 + jnp.exp(-x))


def _layer_norm(y, g, b):
    mu = jnp.mean(y, axis=-1, keepdims=True)
    d = y - mu
    var = jnp.mean(d * d, axis=-1, keepdims=True)
    return d * lax.rsqrt(var + LN_EPS) * g + b


def _proj_kernel(x_ref, w_ref, *out_refs, segs, col_chunk):
    xb = x_ref[...].astype(BF16)
    for o_ref, (start, width) in zip(out_refs, segs):
        for c0 in range(0, width, col_chunk):
            c1 = min(c0 + col_chunk, width)
            acc = jnp.dot(xb, w_ref[:, start + c0:start + c1], preferred_element_type=F32)
            o_ref[:, c0:c1] = acc.astype(o_ref.dtype)


def _proj(x, w, segs, dtypes, tm):
    m, k = x.shape
    n = w.shape[1]
    assert m % tm == 0
    return pl.pallas_call(
        functools.partial(_proj_kernel, segs=segs, col_chunk=512),
        grid=(m // tm,),
        in_specs=[pl.BlockSpec((tm, k), lambda i: (i, 0)),
                  pl.BlockSpec((k, n), lambda i: (0, 0))],
        out_specs=[pl.BlockSpec((tm, wd), lambda i: (i, 0)) for _, wd in segs],
        out_shape=[jax.ShapeDtypeStruct((m, wd), dt) for (_, wd), dt in zip(segs, dtypes)],
        compiler_params=_cparams(1),
        name="proj",
    )(x, w)


ROPE_HALF = SWA_DH // 2
ROPE_PACK = V7X_LANES // ROPE_HALF


def _expand_rope(dense, dst_ref, signed):
    n = dense.shape[0]
    lane = lax.broadcasted_iota(jnp.int32, dense.shape, 1)
    group = lane // ROPE_HALF
    sign = jnp.where((group & 1) == 0, -1.0, 1.0)
    for k in range(ROPE_PACK):
        x = jnp.where(group == k, dense, 0.0)
        x = x + pltpu.roll(x, ROPE_HALF, axis=1)
        x = x + pltpu.roll(x, 2 * ROPE_HALF, axis=1)
        if signed:
            x = x * sign
        dst_ref[pl.ds(k, n, stride=ROPE_PACK), :] = x


def _rope_table_kernel(pos_ref, inv_ref, cos_ref, sin_ref):
    ang = pos_ref[...].astype(F32) * inv_ref[...]
    _expand_rope(jnp.cos(ang), cos_ref, False)
    _expand_rope(jnp.sin(ang), sin_ref, True)


def _rope_tables(positions):
    b, s = positions.shape
    inv_freq = ROPE_THETA ** (-jnp.arange(0, SWA_DH, 2, dtype=F32) / SWA_DH)
    pos_rep = jnp.broadcast_to(positions[..., None], (b, s, ROPE_HALF)).reshape(b * s // ROPE_PACK, V7X_LANES)
    inv_rep = jnp.tile(inv_freq, ROPE_PACK)[None, :]
    rows = pos_rep.shape[0]
    tr = min(rows, 1024)
    assert rows % tr == 0
    cos_t, sin_t = pl.pallas_call(
        _rope_table_kernel,
        grid=(rows // tr,),
        in_specs=[pl.BlockSpec((tr, V7X_LANES), lambda i: (i, 0)),
                  pl.BlockSpec((1, V7X_LANES), lambda i: (0, 0))],
        out_specs=[pl.BlockSpec((tr * ROPE_PACK, V7X_LANES), lambda i: (i, 0))] * 2,
        out_shape=[jax.ShapeDtypeStruct((rows * ROPE_PACK, V7X_LANES), F32)] * 2,
        compiler_params=_cparams(1),
        name="rope_tables",
    )(pos_rep, inv_rep)
    return cos_t.reshape(b, s, V7X_LANES), sin_t.reshape(b, s, V7X_LANES)


def _rope_pairs(x, cos_t, sin_t):
    lane = lax.broadcasted_iota(jnp.int32, x.shape, 1)
    first_half = (lane & (SWA_DH - 1)) < (SWA_DH // 2)
    swapped = jnp.where(first_half,
                        pltpu.roll(x, V7X_LANES - SWA_DH // 2, axis=1),
                        pltpu.roll(x, SWA_DH // 2, axis=1))
    return x * cos_t + swapped * sin_t


A_QM_COL = 4 * DN_W
A_AB_COL = A_QM_COL + MEM_W
A_W_COLS = A_AB_COL + 2 * V7X_LANES
CONV_PAD = 8
CONV_COLS = 2 * V7X_LANES


def _proj_a_kernel(h_ref, w_ref, cw_ref, alog_ref, dtb_ref,
                   qkv_ref, z_ref, qm_ref, beta_ref, gc_ref, gr_ref, ext_ref, carry_ref):
    tm = h_ref.shape[1]
    lanes = V7X_LANES

    @pl.when(pl.program_id(1) == 0)
    def _():
        carry_ref[...] = jnp.zeros_like(carry_ref)

    xb = h_ref[0].astype(BF16)

    def matmul(c0, width):
        return jnp.dot(xb, w_ref[:, c0:c0 + width], preferred_element_type=F32)

    def conv_block(c0):
        for half in range(CONV_COLS // lanes):
            cc = c0 + half * lanes
            y = cw_ref[CONV_WIDTH - 1:CONV_WIDTH, cc:cc + lanes] * ext_ref[CONV_PAD:CONV_PAD + tm, cc:cc + lanes]
            for j in range(CONV_WIDTH - 1):
                r0 = CONV_PAD - (CONV_WIDTH - 1) + j
                y = y + cw_ref[j:j + 1, cc:cc + lanes] * ext_ref[r0:r0 + tm, cc:cc + lanes]
            y = y * _sigmoid(y)
            if cc < 2 * DN_W:
                y = y * lax.rsqrt(jnp.sum(y * y, axis=-1, keepdims=True) + NORM_EPS)
            qkv_ref[0, :, cc:cc + lanes] = y.astype(qkv_ref.dtype)

    starts = list(range(0, 3 * DN_W, CONV_COLS))
    ext_ref[0:CONV_PAD, :] = carry_ref[...]
    for c0 in starts:
        acc = matmul(c0, CONV_COLS)
        ext_ref[CONV_PAD:CONV_PAD + tm, c0:c0 + CONV_COLS] = acc
        carry_ref[:, c0:c0 + CONV_COLS] = acc[tm - CONV_PAD:tm, :]

    for c0 in range(0, DN_W, CONV_COLS):
        z_ref[0, :, c0:c0 + CONV_COLS] = matmul(3 * DN_W + c0, CONV_COLS).astype(z_ref.dtype)
    qm_ref[0] = matmul(A_QM_COL, MEM_W).astype(qm_ref.dtype)

    ab = matmul(A_AB_COL, 2 * lanes)
    for c0 in starts:
        conv_block(c0)
    xg = ab[:, :lanes] + dtb_ref[...]
    softplus = jnp.maximum(xg, 0.0) + jnp.log1p(jnp.exp(-jnp.abs(xg)))
    g = -jnp.exp(alog_ref[...]) * softplus
    beta_ref[0] = _sigmoid(ab[:, lanes:])

    row = lax.broadcasted_iota(jnp.int32, g.shape, 0) & (CHUNK - 1)
    gc = g
    step = 1
    while step < CHUNK:
        gc = gc + jnp.where(row >= step, pltpu.roll(gc, step, axis=0), 0.0)
        step *= 2
    gc_ref[0] = gc

    gct = jnp.transpose(gc)[0:8, :]
    plus = pltpu.roll(pltpu.roll(gct, CHUNK, axis=1), 7, axis=0)
    minus = pltpu.roll(pltpu.roll(gct, tm - CHUNK, axis=1), 1, axis=0)
    lane = lax.broadcasted_iota(jnp.int32, gct.shape, 1)
    low = (lane & (lanes - 1)) < CHUNK
    even = (lax.broadcasted_iota(jnp.int32, gct.shape, 0) & 1) == 0
    gr_ref[0] = jnp.where(even, jnp.where(low, gct, plus), jnp.where(low, minus, gct))


def _proj_a(h, w, conv_w, a_log, dt_bias, tm):
    b, s, d = h.shape
    assert s % tm == 0 and tm % V7X_LANES == 0 and w.shape == (d, A_W_COLS)
    pad = V7X_LANES - DN_HEADS
    alog = jnp.pad(a_log.astype(F32), (0, pad))[None, :]
    dtb = jnp.pad(dt_bias.astype(F32), (0, pad))[None, :]
    cur = lambda bi, i: (bi, i, 0)
    const = lambda bi, i: (0, 0)
    lanes = V7X_LANES
    return pl.pallas_call(
        _proj_a_kernel,
        grid=(b, s // tm),
        in_specs=[
            pl.BlockSpec((1, tm, d), cur),
            pl.BlockSpec((d, A_W_COLS), const),
            pl.BlockSpec((CONV_WIDTH, 3 * DN_W), const),
            pl.BlockSpec((1, lanes), const),
            pl.BlockSpec((1, lanes), const),
        ],
        out_specs=[
            pl.BlockSpec((1, tm, 3 * DN_W), cur),
            pl.BlockSpec((1, tm, DN_W), cur),
            pl.BlockSpec((1, tm, MEM_W), cur),
            pl.BlockSpec((1, tm, lanes), cur),
            pl.BlockSpec((1, tm, lanes), cur),
            pl.BlockSpec((1, 8, tm), lambda bi, i: (bi, 0, i)),
        ],
        out_shape=[
            jax.ShapeDtypeStruct((b, s, 3 * DN_W), BF16),
            jax.ShapeDtypeStruct((b, s, DN_W), BF16),
            jax.ShapeDtypeStruct((b, s, MEM_W), BF16),
            jax.ShapeDtypeStruct((b, s, lanes), F32),
            jax.ShapeDtypeStruct((b, s, lanes), F32),
            jax.ShapeDtypeStruct((b, 8, s), F32),
        ],
        scratch_shapes=[pltpu.VMEM((CONV_PAD + tm, 3 * DN_W), F32),
                        pltpu.VMEM((CONV_PAD, 3 * DN_W), F32)],
        compiler_params=_cparams(2),
        name="proj_a",
    )(h, w, conv_w.astype(F32), alog, dtb)


def _gdn_kernel(qkv_ref, z_ref, beta_ref, gc_ref, gr_ref, nw_ref, o_ref, state_ref):
    ts = qkv_ref.shape[1]
    c = CHUNK
    lanes = V7X_LANES

    @pl.when(pl.program_id(1) == 0)
    def _():
        state_ref[...] = jnp.zeros_like(state_ref)

    lane2 = lax.broadcasted_iota(jnp.int32, (c, 2 * lanes), 1)
    head_a2 = lane2 < lanes
    lane1 = lax.broadcasted_iota(jnp.int32, (c, lanes), 1)
    half_a = lane1 < c
    col = lane1 & (c - 1)
    row = lax.broadcasted_iota(jnp.int32, (c, lanes), 0)
    incl = row >= col
    strict = row > col
    eye = jnp.where(row == col, 1.0, 0.0)
    lane_sq = lax.broadcasted_iota(jnp.int32, (lanes, lanes), 1)
    half_a_sq = lane_sq < c
    nw = nw_ref[...]
    q_scale = DN_DK ** -0.5

    def split_halves(x, mask):
        return jnp.concatenate([jnp.where(mask, x, 0.0), jnp.where(mask, 0.0, x)], axis=0)

    items = [(p, ci) for ci in range(ts // c) for p in range(DN_PAIRS)]
    n_sq = c.bit_length() - 1

    xs, invs, intras, rhss, kdts, qes, cds = [], [], [], [], [], [], []
    for p, ci in items:
        ha, hb = 2 * p, 2 * p + 1
        r0 = ci * c
        qp = qkv_ref[0, r0:r0 + c, 2 * lanes * p:2 * lanes * (p + 1)].astype(F32)
        kp = qkv_ref[0, r0:r0 + c, DN_W + 2 * lanes * p:DN_W + 2 * lanes * (p + 1)].astype(F32)
        vp = qkv_ref[0, r0:r0 + c, 2 * DN_W + 2 * lanes * p:2 * DN_W + 2 * lanes * (p + 1)].astype(F32)
        beta_t = beta_ref[0, r0:r0 + c, :]
        gc_t = gc_ref[0, r0:r0 + c, :]
        b_a, b_b = beta_t[:, ha:ha + 1], beta_t[:, hb:hb + 1]
        g_a, g_b = gc_t[:, ha:ha + 1], gc_t[:, hb:hb + 1]
        beta2 = jnp.where(head_a2, b_a, b_b)
        g2 = jnp.where(head_a2, g_a, g_b)
        g1 = jnp.where(half_a, g_a, g_b)
        g_row = gr_ref[0, 2 * p + (ci % 2):2 * p + (ci % 2) + 1, lanes * (ci // 2):lanes * (ci // 2 + 1)]
        g_last = g2[c - 1:c, :]
        eg = jnp.exp(g2)
        kb = kp * beta2
        vb = vp * beta2
        kbe = kb * eg
        qs = qp * q_scale
        kd = kp * jnp.exp(g_last - g2)
        qes.append(qs * eg)
        cds.append(jnp.exp(g_last))

        r1 = _mm_nt(jnp.concatenate([kb, qs], axis=0), split_halves(kp, head_a2))
        decay = jnp.where(incl, jnp.exp(g1 - g_row), 0.0)
        x = jnp.where(strict, -(r1[:c] * decay), 0.0)
        xs.append(x)
        invs.append(eye + x)
        intras.append(r1[c:] * decay)
        rhss.append(jnp.concatenate([
            jnp.concatenate([vb[:, :lanes], kbe[:, :lanes]], axis=1),
            jnp.concatenate([vb[:, lanes:], kbe[:, lanes:]], axis=1)], axis=0).astype(BF16))
        kdts.append(jnp.transpose(jnp.concatenate([kd[:, :lanes], kd[:, lanes:]], axis=0)))

    xs = [_mm(x, split_halves(x, half_a)) for x in xs]
    for _ in range(n_sq - 2):
        rs = [_mm(jnp.concatenate([x, inv], axis=0), split_halves(x, half_a)) for x, inv in zip(xs, invs)]
        xs = [r[:c] for r in rs]
        invs = [inv + r[c:] for inv, r in zip(invs, rs)]
    invs = [inv + _mm(inv, split_halves(x, half_a)) for x, inv in zip(xs, invs)]

    sols = [_mm(split_halves(inv, half_a), rhs) for inv, rhs in zip(invs, rhss)]
    r3s = [_mm(jnp.concatenate([split_halves(intra, half_a), split_halves(kdt, half_a_sq)], axis=0), sol)
           for intra, kdt, sol in zip(intras, kdts, sols)]

    states = [state_ref[hd] for hd in range(DN_HEADS)]
    for (p, ci), r3, qe, cd in zip(items, r3s, qes, cds):
        r0 = ci * c
        for hx in range(2):
            hd = 2 * p + hx
            lo = hx * lanes
            s_x = states[hd]
            o0 = r3[hx * c:(hx + 1) * c, :lanes]
            iw = r3[hx * c:(hx + 1) * c, lanes:]
            n0 = r3[2 * c + hx * lanes:2 * c + (hx + 1) * lanes, :lanes]
            kw = r3[2 * c + hx * lanes:2 * c + (hx + 1) * lanes, lanes:]
            r4 = _mm(jnp.concatenate([qe[:, lo:lo + lanes] - iw, -kw], axis=0), s_x)
            o = r4[:c] + o0
            states[hd] = cd[:, lo:lo + lanes] * s_x + n0 + r4[c:]
            zt = z_ref[0, r0:r0 + c, hd * lanes:(hd + 1) * lanes].astype(F32)
            on = o * lax.rsqrt(jnp.mean(o * o, axis=-1, keepdims=True) + NORM_EPS) * nw
            o_ref[0, r0:r0 + c, hd * lanes:(hd + 1) * lanes] = (on * (zt * _sigmoid(zt))).astype(o_ref.dtype)
    for hd in range(DN_HEADS):
        state_ref[hd] = states[hd]


def _gdn(qkv, z, beta, gc, gr, norm_w, ts):
    b, s, w = qkv.shape
    return pl.pallas_call(
        _gdn_kernel,
        grid=(b, s // ts),
        in_specs=[
            pl.BlockSpec((1, ts, w), lambda bi, i: (bi, i, 0)),
            pl.BlockSpec((1, ts, DN_W), lambda bi, i: (bi, i, 0)),
            pl.BlockSpec((1, ts, V7X_LANES), lambda bi, i: (bi, i, 0)),
            pl.BlockSpec((1, ts, V7X_LANES), lambda bi, i: (bi, i, 0)),
            pl.BlockSpec((1, 8, ts), lambda bi, i: (bi, 0, i)),
            pl.BlockSpec((1, V7X_LANES), lambda bi, i: (0, 0)),
        ],
        out_specs=pl.BlockSpec((1, ts, DN_W), lambda bi, i: (bi, i, 0)),
        out_shape=jax.ShapeDtypeStruct((b, s, DN_W), BF16),
        scratch_shapes=[pltpu.VMEM((DN_HEADS, DN_DK, DN_DK), F32)],
        compiler_params=_cparams(2),
        name="gdn",
    )(qkv, z, beta, gc, gr, norm_w.astype(F32)[None, :])


def _swa_kernel(sink_ref, q_ref, kv_ref, kvh_ref, cos_ref, sin_ref, cosh_ref, sinh_ref, o_ref,
                kvar_ref, vt_ref):
    tq = q_ref.shape[1]
    lanes = V7X_LANES
    w = WINDOW
    first_key = jnp.where(pl.program_id(1) == 0, w, 0)

    kv_all = jnp.concatenate([kvh_ref[0], kv_ref[0]], axis=0)
    cos_all = jnp.concatenate([cosh_ref[0], cos_ref[0]], axis=0)
    sin_all = jnp.concatenate([sinh_ref[0], sin_ref[0]], axis=0)
    k_rot = _rope_pairs(kv_all[:, :lanes], cos_all, sin_all)
    v_all = kv_all[:, lanes:]
    lane = lax.broadcasted_iota(jnp.int32, k_rot.shape, 1)
    low = lane < SWA_DH
    k_sw = pltpu.roll(k_rot, SWA_DH, axis=1)
    kvar_ref[0] = jnp.where(low, k_rot, 0.0).astype(BF16)
    kvar_ref[1] = jnp.where(low, 0.0, k_sw).astype(BF16)
    kvar_ref[2] = jnp.where(low, k_sw, 0.0).astype(BF16)
    kvar_ref[3] = jnp.where(low, 0.0, k_rot).astype(BF16)
    v_sw = pltpu.roll(v_all, SWA_DH, axis=1)
    ones_hi = jnp.where(lane == SWA_DH, 1.0, 0.0)
    ones_lo = jnp.where(lane == 0, 1.0, 0.0)
    vt_ref[0] = jnp.transpose(jnp.where(low, v_all, ones_hi)).astype(BF16)
    vt_ref[1] = jnp.transpose(jnp.where(low, ones_lo, v_sw)).astype(BF16)
    vt_ref[2] = jnp.transpose(jnp.where(low, v_sw, ones_hi)).astype(BF16)
    vt_ref[3] = jnp.transpose(jnp.where(low, ones_lo, v_all)).astype(BF16)
    sum_row = (SWA_DH, 0)

    q_scale = SWA_DH ** -0.5 * LOG2_E
    q_t = []
    for p in range(SWA_PAIRS):
        qf = q_ref[0, :, p * lanes:(p + 1) * lanes].astype(F32)
        q_t.append(jnp.transpose(_rope_pairs(qf, cos_ref[0], sin_ref[0]) * q_scale).astype(BF16))

    cols3 = SWA_PAIRS_PER_KV * w
    kj = lax.broadcasted_iota(jnp.int32, (2 * w, cols3), 0)
    qi = lax.broadcasted_iota(jnp.int32, (2 * w, cols3), 1) & (w - 1)
    dist = qi + w - kj
    band = (dist >= 0) & (dist < w)
    neg_inf = jnp.float32(-jnp.inf)
    bias_band = jnp.where(band, 0.0, neg_inf)
    bias_first = jnp.where(band & (kj >= first_key), 0.0, neg_inf)
    head_slot = lax.broadcasted_iota(jnp.int32, (1, cols3), 1) // w
    row_low = lax.broadcasted_iota(jnp.int32, (lanes, cols3), 0) < SWA_DH
    combos = [(g, e) for g in range(SWA_KV_HEADS) for e in range(2)]
    group_pairs = [[SWA_PAIRS_PER_KV * g + t for t in range(SWA_PAIRS_PER_KV)] for g in range(SWA_KV_HEADS)]

    def scores(j):
        r0 = j * w
        out = []
        for g, e in combos:
            qs = jnp.concatenate([q_t[p][:, r0:r0 + w] for p in group_pairs[g]], axis=1)
            out.append(jnp.dot(kvar_ref[2 * g + e, r0:r0 + 2 * w, :], qs, preferred_element_type=F32))
        return out

    def finish(j, ss):
        r0 = j * w
        bias = bias_first if j == 0 else bias_band
        pes, sink_terms = [], []
        for (g, e), s in zip(combos, ss):
            s = s + bias
            hs = [2 * p + e for p in group_pairs[g]]
            sink = jnp.where(head_slot == 0, sink_ref[hs[0]],
                             jnp.where(head_slot == 1, sink_ref[hs[1]], sink_ref[hs[2]])) * LOG2_E
            m = jnp.maximum(jnp.max(s, axis=0, keepdims=True), sink)
            pes.append(jnp.exp2(s - m).astype(BF16))
            sink_terms.append(jnp.exp2(sink - m))
        pvs = [jnp.dot(vt_ref[2 * g + e, :, r0:r0 + 2 * w], pe, preferred_element_type=F32)
               for (g, e), pe in zip(combos, pes)]
        for g in range(SWA_KV_HEADS):
            normed = []
            for e in range(2):
                pv = pvs[2 * g + e]
                den = pv[sum_row[e]:sum_row[e] + 1, :] + sink_terms[2 * g + e]
                normed.append(pv * (1.0 / den))
            acc = jnp.where(row_low, normed[0], normed[1])
            for t, p in enumerate(group_pairs[g]):
                o_ref[0, r0:r0 + w, p * lanes:(p + 1) * lanes] = (
                    jnp.transpose(acc[:, t * w:(t + 1) * w]).astype(o_ref.dtype))

    n_blocks = tq // w
    ss = scores(0)
    for j in range(n_blocks):
        nxt = scores(j + 1) if j + 1 < n_blocks else None
        finish(j, ss)
        ss = nxt


def _swa(q_pre, kv_sh, cos_t, sin_t, sinks, tq):
    b, s, _ = q_pre.shape
    assert s % tq == 0 and tq % WINDOW == 0
    hb = tq // WINDOW
    cur = lambda bi, i: (bi, i, 0)
    halo = lambda bi, i: (bi, jnp.maximum(i * hb - 1, 0), 0)
    lanes = V7X_LANES
    return pl.pallas_call(
        _swa_kernel,
        grid=(b, s // tq),
        in_specs=[
            pl.BlockSpec(memory_space=pltpu.SMEM),
            pl.BlockSpec((1, tq, SWA_Q_W), cur),
            pl.BlockSpec((1, tq, 2 * lanes), cur),
            pl.BlockSpec((1, WINDOW, 2 * lanes), halo),
            pl.BlockSpec((1, tq, lanes), cur),
            pl.BlockSpec((1, tq, lanes), cur),
            pl.BlockSpec((1, WINDOW, lanes), halo),
            pl.BlockSpec((1, WINDOW, lanes), halo),
        ],
        out_specs=pl.BlockSpec((1, tq, SWA_Q_W), cur),
        out_shape=jax.ShapeDtypeStruct((b, s, SWA_Q_W), BF16),
        scratch_shapes=[pltpu.VMEM((4, WINDOW + tq, lanes), BF16),
                        pltpu.VMEM((4, lanes, WINDOW + tq), BF16)],
        compiler_params=_cparams(2),
        name="swa",
    )(sinks.astype(F32), q_pre, kv_sh, kv_sh, cos_t, sin_t, cos_t, sin_t)


def _mix_out_kernel(o_ref, qm_ref, mk_ref, mvt_ref, wo_ref, h_ref, g_ref, b_ref, out_ref):
    mk = mk_ref[0].astype(F32)
    mvt = mvt_ref[0]
    lane_head = lax.broadcasted_iota(jnp.int32, mk.shape, 1) // MEM_DH
    heads = range(MEM_HEADS)
    khs = [jnp.where(lane_head == hd, mk, 0.0).astype(BF16) for hd in heads]
    ones = jnp.ones((BF16_SUBLANES, mvt.shape[1]), BF16)
    vhs = [jnp.concatenate([mvt[hd * MEM_DH:(hd + 1) * MEM_DH, :], ones], axis=0) for hd in heads]
    dn = o_ref.shape[2]
    qm = qm_ref[0].astype(F32) * (MEM_DH ** -0.5 * LOG2_E)
    qm_t = jnp.transpose(qm).astype(BF16)
    ss = [jnp.dot(kh, qm_t, preferred_element_type=F32) for kh in khs]
    mix = jnp.dot(o_ref[0], wo_ref[0:dn, :], preferred_element_type=F32)
    pes = [jnp.exp2(s - jnp.max(s, axis=0, keepdims=True)).astype(BF16) for s in ss]
    parts = []
    for vh, pe in zip(vhs, pes):
        pv = jnp.dot(vh, pe, preferred_element_type=F32)
        parts.append(pv[:MEM_DH] * (1.0 / pv[MEM_DH:MEM_DH + 1]))
    mo = jnp.transpose(jnp.concatenate(parts, axis=0))
    mix = mix + jnp.dot(mo.astype(BF16), wo_ref[dn:, :], preferred_element_type=F32)
    out_ref[0] = _layer_norm(DN_ALPHA * h_ref[0] + mix, g_ref[...], b_ref[...])


def _mix_out(o, qm, mk, mvt, wo, h, ln_g, ln_b, tm):
    b, s, d = h.shape
    m_tok = mk.shape[1]
    cur = lambda bi, i: (bi, i, 0)
    per_b = lambda bi, i: (bi, 0, 0)
    const = lambda bi, i: (0, 0)
    return pl.pallas_call(
        _mix_out_kernel,
        grid=(b, s // tm),
        in_specs=[
            pl.BlockSpec((1, tm, o.shape[2]), cur),
            pl.BlockSpec((1, tm, MEM_W), cur),
            pl.BlockSpec((1, m_tok, MEM_W), per_b),
            pl.BlockSpec((1, MEM_W, m_tok), per_b),
            pl.BlockSpec(wo.shape, const),
            pl.BlockSpec((1, tm, d), cur),
            pl.BlockSpec((1, d), const),
            pl.BlockSpec((1, d), const),
        ],
        out_specs=pl.BlockSpec((1, tm, d), cur),
        out_shape=jax.ShapeDtypeStruct((b, s, d), F32),
        compiler_params=_cparams(2),
        name="mix_out",
    )(o, qm, mk, mvt, wo, h, ln_g[None, :], ln_b[None, :])


def _mlp_kernel(x_ref, up_ref, dn_ref, g_ref, b_ref, o_ref, *, hid_chunk):
    x = x_ref[...]
    xb = x.astype(BF16)
    hidden = up_ref.shape[1]
    acc = jnp.zeros(x.shape, F32)
    for j in range(0, hidden, hid_chunk):
        hid = jnp.dot(xb, up_ref[:, j:j + hid_chunk], preferred_element_type=F32)
        hid = jnp.square(jnp.maximum(hid, 0.0))
        acc = acc + jnp.dot(hid.astype(BF16), dn_ref[j:j + hid_chunk, :], preferred_element_type=F32)
    o_ref[...] = _layer_norm(DN_ALPHA * x + acc, g_ref[...], b_ref[...])


def _mlp(x, w_up, w_down, ln_g, ln_b, tm):
    m, d = x.shape
    hidden = w_up.shape[1]
    return pl.pallas_call(
        functools.partial(_mlp_kernel, hid_chunk=512),
        grid=(m // tm,),
        in_specs=[
            pl.BlockSpec((tm, d), lambda i: (i, 0)),
            pl.BlockSpec((d, hidden), lambda i: (0, 0), pipeline_mode=pl.Buffered(1)),
            pl.BlockSpec((hidden, d), lambda i: (0, 0), pipeline_mode=pl.Buffered(1)),
            pl.BlockSpec((1, d), lambda i: (0, 0)),
            pl.BlockSpec((1, d), lambda i: (0, 0)),
        ],
        out_specs=pl.BlockSpec((tm, d), lambda i: (i, 0)),
        out_shape=jax.ShapeDtypeStruct((m, d), F32),
        compiler_params=_cparams(1),
        name="mlp",
    )(x, w_up, w_down, ln_g[None, :], ln_b[None, :])


def _a_weights_kernel(w_ref, o_ref):
    lanes = V7X_LANES
    ab0 = A_QM_COL
    o_ref[:, 0:A_QM_COL] = w_ref[:, 0:A_QM_COL].astype(BF16)
    o_ref[:, A_QM_COL:A_AB_COL] = w_ref[:, ab0 + 2 * DN_HEADS:ab0 + 2 * DN_HEADS + MEM_W].astype(BF16)
    first = w_ref[:, ab0:ab0 + lanes]
    lane = lax.broadcasted_iota(jnp.int32, first.shape, 1)
    a = jnp.where(lane < DN_HEADS, first, 0.0)
    b = jnp.where(lane < DN_HEADS, pltpu.roll(first, lanes - DN_HEADS, axis=1), 0.0)
    o_ref[:, A_AB_COL:A_AB_COL + lanes] = a.astype(BF16)
    o_ref[:, A_AB_COL + lanes:] = b.astype(BF16)


def _a_proj_weights(w_in_all, layer):
    _, k, n = w_in_all.shape
    assert n == A_QM_COL + 2 * DN_HEADS + MEM_W
    tk = 256
    return pl.pallas_call(
        _a_weights_kernel,
        grid=(k // tk,),
        in_specs=[pl.BlockSpec((None, tk, n), lambda i: (layer, i, 0))],
        out_specs=pl.BlockSpec((tk, A_W_COLS), lambda i: (i, 0)),
        out_shape=jax.ShapeDtypeStruct((k, A_W_COLS), BF16),
        compiler_params=_cparams(1),
        name="a_weights",
    )(w_in_all)


def _tile(n, pref):
    t = min(n, pref)
    assert n % t == 0
    return t


def kernel(x, mem, positions, a_w_in, a_conv_w, a_A_log, a_dt_bias, a_norm_w, b_w_in, b_sinks,
           w_kv_shared, mem_w_kv, w_o, mlp_w_up, mlp_w_down, ln_g, ln_b):
    b, s, d = x.shape
    t = b * s
    m_tok = mem.shape[1]
    tm = _tile(s, ROW_TILE)
    mem2 = mem.reshape(b * m_tok, d)
    h = x
    kv_sh = None
    cos_t = sin_t = None
    for layer in range(DEPTH):
        mk, mv = _proj(mem2, mem_w_kv[layer].astype(BF16), ((0, MEM_W), (MEM_W, MEM_W)), (BF16, BF16),
                       b * m_tok)
        mk = mk.reshape(b, m_tok, MEM_W)
        mvt = jnp.swapaxes(mv.reshape(b, m_tok, MEM_W), 1, 2)
        h2 = h.reshape(t, d)
        if layer < N_A:
            qkv, z, qm, beta, gc, gr = _proj_a(h, _a_proj_weights(a_w_in, layer), a_conv_w[layer],
                                               a_A_log[layer], a_dt_bias[layer], tm)
            ts = _tile(s, GDN_TILE)
            o = _gdn(qkv, z, beta, gc, gr, a_norm_w[layer], ts)
        else:
            j = layer - N_A
            q_pre, qm = _proj(h2, b_w_in[j].astype(BF16), ((0, SWA_Q_W), (SWA_Q_W, MEM_W)), (BF16, BF16), tm)
            qm = qm.reshape(b, s, -1)
            o = _swa(q_pre.reshape(b, s, -1), kv_sh, cos_t, sin_t, b_sinks[j], _tile(s, SWA_TILE))
        h = _mix_out(o, qm, mk, mvt, w_o[layer].astype(BF16), h,
                     ln_g[layer, 0], ln_b[layer, 0], tm)
        h = _mlp(h.reshape(t, d), mlp_w_up[layer].astype(BF16), mlp_w_down[layer].astype(BF16),
                 ln_g[layer, 1], ln_b[layer, 1], tm).reshape(b, s, d)
        if layer == N_A - 1:
            (kv_pre,) = _proj(h.reshape(t, d), w_kv_shared.astype(BF16), ((0, 2 * V7X_LANES),), (F32,), tm)
            kv_sh = kv_pre.reshape(b, s, -1)
            cos_t, sin_t = _rope_tables(positions)
    return h
```

```python
import functools

import jax
import jax.numpy as jnp
from jax import lax
from jax.experimental import pallas as pl
from jax.experimental.pallas import tpu as pltpu

F32 = jnp.float32
BF16 = jnp.bfloat16

V7X_LANES = 128
V7X_VMEM_LIMIT_BYTES = 56 * 1024 * 1024
BF16_SUBLANES = 16

DEPTH = 4
N_A = DEPTH // 2
MEM_HEADS = 4
MEM_DH = 64
MEM_W = MEM_HEADS * MEM_DH
DN_DK = 128
DN_HEADS = 6
DN_PAIRS = DN_HEADS // 2
DN_W = DN_HEADS * DN_DK
CONV_WIDTH = 4
CHUNK = 64
SWA_DH = 64
SWA_HEADS = 12
SWA_KV_HEADS = 2
SWA_Q_W = SWA_HEADS * SWA_DH
SWA_PAIRS = SWA_Q_W // V7X_LANES
SWA_PAIRS_PER_KV = SWA_PAIRS // SWA_KV_HEADS
WINDOW = 128
ROPE_THETA = 10000.0
LN_EPS = 1e-5
NORM_EPS = 1e-6
DN_ALPHA = (2.0 * DEPTH) ** 0.25
LOG2_E = 1.4426950408889634

ROW_TILE = 1024
GDN_TILE = 512
GDN_GROUP_CHUNKS = 8
SWA_TILE = 512


def _cparams(n_grid_axes):
    return pltpu.CompilerParams(
        dimension_semantics=("arbitrary",) * n_grid_axes,
        vmem_limit_bytes=V7X_VMEM_LIMIT_BYTES)


def _mm(a, b):
    return jnp.dot(a.astype(BF16), b.astype(BF16), preferred_element_type=F32)


def _mm_nt(a, b):
    return lax.dot_general(a.astype(BF16), b.astype(BF16), (((1,), (1,)), ((), ())),
                           preferred_element_type=F32)


def _sigmoid(x):
    return 1.0 / (1.0 + jnp.exp(-x))


def _layer_norm(y, g, b):
    mu = jnp.mean(y, axis=-1, keepdims=True)
    d = y - mu
    var = jnp.mean(d * d, axis=-1, keepdims=True)
    return d * lax.rsqrt(var + LN_EPS) * g + b


def _proj_kernel(x_ref, w_ref, *out_refs, segs, col_chunk):
    xb = x_ref[...].astype(BF16)
    for o_ref, (start, width) in zip(out_refs, segs):
        for c0 in range(0, width, col_chunk):
            c1 = min(c0 + col_chunk, width)
            acc = jnp.dot(xb, w_ref[:, start + c0:start + c1], preferred_element_type=F32)
            o_ref[:, c0:c1] = acc.astype(o_ref.dtype)


def _proj(x, w, segs, dtypes, tm):
    m, k = x.shape
    n = w.shape[1]
    assert m % tm == 0
    return pl.pallas_call(
        functools.partial(_proj_kernel, segs=segs, col_chunk=512),
        grid=(m // tm,),
        in_specs=[pl.BlockSpec((tm, k), lambda i: (i, 0)),
                  pl.BlockSpec((k, n), lambda i: (0, 0))],
        out_specs=[pl.BlockSpec((tm, wd), lambda i: (i, 0)) for _, wd in segs],
        out_shape=[jax.ShapeDtypeStruct((m, wd), dt) for (_, wd), dt in zip(segs, dtypes)],
        compiler_params=_cparams(1),
        name="proj",
    )(x, w)


ROPE_HALF = SWA_DH // 2
ROPE_PACK = V7X_LANES // ROPE_HALF


def _expand_rope(dense, dst_ref, signed):
    n = dense.shape[0]
    lane = lax.broadcasted_iota(jnp.int32, dense.shape, 1)
    group = lane // ROPE_HALF
    sign = jnp.where((group & 1) == 0, -1.0, 1.0)
    for k in range(ROPE_PACK):
        x = jnp.where(group == k, dense, 0.0)
        x = x + pltpu.roll(x, ROPE_HALF, axis=1)
        x = x + pltpu.roll(x, 2 * ROPE_HALF, axis=1)
        if signed:
            x = x * sign
        dst_ref[pl.ds(k, n, stride=ROPE_PACK), :] = x


def _rope_table_kernel(pos_ref, inv_ref, cos_ref, sin_ref):
    ang = pos_ref[...].astype(F32) * inv_ref[...]
    _expand_rope(jnp.cos(ang), cos_ref, False)
    _expand_rope(jnp.sin(ang), sin_ref, True)


def _rope_tables(positions):
    b, s = positions.shape
    inv_freq = ROPE_THETA ** (-jnp.arange(0, SWA_DH, 2, dtype=F32) / SWA_DH)
    pos_rep = jnp.broadcast_to(positions[..., None], (b, s, ROPE_HALF)).reshape(b * s // ROPE_PACK, V7X_LANES)
    inv_rep = jnp.tile(inv_freq, ROPE_PACK)[None, :]
    rows = pos_rep.shape[0]
    tr = min(rows, 1024)
    assert rows % tr == 0
    cos_t, sin_t = pl.pallas_call(
        _rope_table_kernel,
        grid=(rows // tr,),
        in_specs=[pl.BlockSpec((tr, V7X_LANES), lambda i: (i, 0)),
                  pl.BlockSpec((1, V7X_LANES), lambda i: (0, 0))],
        out_specs=[pl.BlockSpec((tr * ROPE_PACK, V7X_LANES), lambda i: (i, 0))] * 2,
        out_shape=[jax.ShapeDtypeStruct((rows * ROPE_PACK, V7X_LANES), F32)] * 2,
        compiler_params=_cparams(1),
        name="rope_tables",
    )(pos_rep, inv_rep)
    return cos_t.reshape(b, s, V7X_LANES), sin_t.reshape(b, s, V7X_LANES)


def _rope_pairs(x, cos_t, sin_t):
    lane = lax.broadcasted_iota(jnp.int32, x.shape, 1)
    first_half = (lane & (SWA_DH - 1)) < (SWA_DH // 2)
    swapped = jnp.where(first_half,
                        pltpu.roll(x, V7X_LANES - SWA_DH // 2, axis=1),
                        pltpu.roll(x, SWA_DH // 2, axis=1))
    return x * cos_t + swapped * sin_t


A_QM_COL = 4 * DN_W
A_AB_COL = A_QM_COL + MEM_W
A_W_COLS = A_AB_COL + 2 * V7X_LANES
CONV_PAD = 8
CONV_COLS = 2 * V7X_LANES


def _proj_a_kernel(h_ref, w_ref, cw_ref, alog_ref, dtb_ref,
                   qkv_ref, z_ref, qm_ref, beta_ref, gc_ref, gr_ref, ext_ref, carry_ref):
    tm = h_ref.shape[1]
    lanes = V7X_LANES

    @pl.when(pl.program_id(1) == 0)
    def _():
        carry_ref[...] = jnp.zeros_like(carry_ref)

    xb = h_ref[0].astype(BF16)

    def matmul(c0, width):
        return jnp.dot(xb, w_ref[:, c0:c0 + width], preferred_element_type=F32)

    def conv_block(c0):
        for half in range(CONV_COLS // lanes):
            cc = c0 + half * lanes
            y = cw_ref[CONV_WIDTH - 1:CONV_WIDTH, cc:cc + lanes] * ext_ref[CONV_PAD:CONV_PAD + tm, cc:cc + lanes]
            for j in range(CONV_WIDTH - 1):
                r0 = CONV_PAD - (CONV_WIDTH - 1) + j
                y = y + cw_ref[j:j + 1, cc:cc + lanes] * ext_ref[r0:r0 + tm, cc:cc + lanes]
            y = y * _sigmoid(y)
            if cc < 2 * DN_W:
                y = y * lax.rsqrt(jnp.sum(y * y, axis=-1, keepdims=True) + NORM_EPS)
            qkv_ref[0, :, cc:cc + lanes] = y.astype(qkv_ref.dtype)

    starts = list(range(0, 3 * DN_W, CONV_COLS))
    ext_ref[0:CONV_PAD, :] = carry_ref[...]
    for c0 in starts:
        acc = matmul(c0, CONV_COLS)
        ext_ref[CONV_PAD:CONV_PAD + tm, c0:c0 + CONV_COLS] = acc
        carry_ref[:, c0:c0 + CONV_COLS] = acc[tm - CONV_PAD:tm, :]

    for c0 in range(0, DN_W, CONV_COLS):
        z_ref[0, :, c0:c0 + CONV_COLS] = matmul(3 * DN_W + c0, CONV_COLS).astype(z_ref.dtype)
    qm_ref[0] = matmul(A_QM_COL, MEM_W).astype(qm_ref.dtype)

    ab = matmul(A_AB_COL, 2 * lanes)
    for c0 in starts:
        conv_block(c0)
    xg = ab[:, :lanes] + dtb_ref[...]
    softplus = jnp.maximum(xg, 0.0) + jnp.log1p(jnp.exp(-jnp.abs(xg)))
    g = -jnp.exp(alog_ref[...]) * softplus
    beta_ref[0] = _sigmoid(ab[:, lanes:])

    row = lax.broadcasted_iota(jnp.int32, g.shape, 0) & (CHUNK - 1)
    gc = g
    step = 1
    while step < CHUNK:
        gc = gc + jnp.where(row >= step, pltpu.roll(gc, step, axis=0), 0.0)
        step *= 2
    gc_ref[0] = gc

    gct = jnp.transpose(gc)[0:8, :]
    plus = pltpu.roll(pltpu.roll(gct, CHUNK, axis=1), 7, axis=0)
    minus = pltpu.roll(pltpu.roll(gct, tm - CHUNK, axis=1), 1, axis=0)
    lane = lax.broadcasted_iota(jnp.int32, gct.shape, 1)
    low = (lane & (lanes - 1)) < CHUNK
    even = (lax.broadcasted_iota(jnp.int32, gct.shape, 0) & 1) == 0
    gr_ref[0] = jnp.where(even, jnp.where(low, gct, plus), jnp.where(low, minus, gct))


def _proj_a(h, w, conv_w, a_log, dt_bias, tm):
    b, s, d = h.shape
    assert s % tm == 0 and tm % V7X_LANES == 0 and w.shape == (d, A_W_COLS)
    pad = V7X_LANES - DN_HEADS
    alog = jnp.pad(a_log.astype(F32), (0, pad))[None, :]
    dtb = jnp.pad(dt_bias.astype(F32), (0, pad))[None, :]
    cur = lambda bi, i: (bi, i, 0)
    const = lambda bi, i: (0, 0)
    lanes = V7X_LANES
    return pl.pallas_call(
        _proj_a_kernel,
        grid=(b, s // tm),
        in_specs=[
            pl.BlockSpec((1, tm, d), cur),
            pl.BlockSpec((d, A_W_COLS), const),
            pl.BlockSpec((CONV_WIDTH, 3 * DN_W), const),
            pl.BlockSpec((1, lanes), const),
            pl.BlockSpec((1, lanes), const),
        ],
        out_specs=[
            pl.BlockSpec((1, tm, 3 * DN_W), cur),
            pl.BlockSpec((1, tm, DN_W), cur),
            pl.BlockSpec((1, tm, MEM_W), cur),
            pl.BlockSpec((1, tm, lanes), cur),
            pl.BlockSpec((1, tm, lanes), cur),
            pl.BlockSpec((1, 8, tm), lambda bi, i: (bi, 0, i)),
        ],
        out_shape=[
            jax.ShapeDtypeStruct((b, s, 3 * DN_W), BF16),
            jax.ShapeDtypeStruct((b, s, DN_W), BF16),
            jax.ShapeDtypeStruct((b, s, MEM_W), BF16),
            jax.ShapeDtypeStruct((b, s, lanes), F32),
            jax.ShapeDtypeStruct((b, s, lanes), F32),
            jax.ShapeDtypeStruct((b, 8, s), F32),
        ],
        scratch_shapes=[pltpu.VMEM((CONV_PAD + tm, 3 * DN_W), F32),
                        pltpu.VMEM((CONV_PAD, 3 * DN_W), F32)],
        compiler_params=_cparams(2),
        name="proj_a",
    )(h, w, conv_w.astype(F32), alog, dtb)


def _gdn_kernel(qkv_ref, z_ref, beta_ref, gc_ref, gr_ref, nw_ref, o_ref, state_ref):
    ts = qkv_ref.shape[1]
    c = CHUNK
    lanes = V7X_LANES

    @pl.when(pl.program_id(1) == 0)
    def _():
        state_ref[...] = jnp.zeros_like(state_ref)

    lane2 = lax.broadcasted_iota(jnp.int32, (c, 2 * lanes), 1)
    head_a2 = lane2 < lanes
    lane1 = lax.broadcasted_iota(jnp.int32, (c, lanes), 1)
    half_a = lane1 < c
    col = lane1 & (c - 1)
    row = lax.broadcasted_iota(jnp.int32, (c, lanes), 0)
    incl = row >= col
    strict = row > col
    eye = jnp.where(row == col, 1.0, 0.0)
    lane_sq = lax.broadcasted_iota(jnp.int32, (lanes, lanes), 1)
    half_a_sq = lane_sq < c
    nw = nw_ref[...]
    q_scale = DN_DK ** -0.5

    def split_halves(x, mask):
        return jnp.concatenate([jnp.where(mask, x, 0.0), jnp.where(mask, 0.0, x)], axis=0)

    n_sq = c.bit_length() - 1
    states = [state_ref[hd] for hd in range(DN_HEADS)]

    def process(items):
        xs, invs, intras, rhss, kdts, qes, cds = [], [], [], [], [], [], []
        for p, ci in items:
            ha, hb = 2 * p, 2 * p + 1
            r0 = ci * c
            qp = qkv_ref[0, r0:r0 + c, 2 * lanes * p:2 * lanes * (p + 1)].astype(F32)
            kp = qkv_ref[0, r0:r0 + c, DN_W + 2 * lanes * p:DN_W + 2 * lanes * (p + 1)].astype(F32)
            vp = qkv_ref[0, r0:r0 + c, 2 * DN_W + 2 * lanes * p:2 * DN_W + 2 * lanes * (p + 1)].astype(F32)
            beta_t = beta_ref[0, r0:r0 + c, :]
            gc_t = gc_ref[0, r0:r0 + c, :]
            b_a, b_b = beta_t[:, ha:ha + 1], beta_t[:, hb:hb + 1]
            g_a, g_b = gc_t[:, ha:ha + 1], gc_t[:, hb:hb + 1]
            beta2 = jnp.where(head_a2, b_a, b_b)
            g2 = jnp.where(head_a2, g_a, g_b)
            g1 = jnp.where(half_a, g_a, g_b)
            g_row = gr_ref[0, 2 * p + (ci % 2):2 * p + (ci % 2) + 1, lanes * (ci // 2):lanes * (ci // 2 + 1)]
            g_last = g2[c - 1:c, :]
            eg = jnp.exp(g2)
            kb = kp * beta2
            vb = vp * beta2
            kbe = kb * eg
            qs = qp * q_scale
            kd = kp * jnp.exp(g_last - g2)
            qes.append(qs * eg)
            cds.append(jnp.exp(g_last))

            r1 = _mm_nt(jnp.concatenate([kb, qs], axis=0), split_halves(kp, head_a2))
            decay = jnp.where(incl, jnp.exp(g1 - g_row), 0.0)
            x = jnp.where(strict, -(r1[:c] * decay), 0.0)
            xs.append(x)
            invs.append(eye + x)
            intras.append(r1[c:] * decay)
            rhss.append(jnp.concatenate([
                jnp.concatenate([vb[:, :lanes], kbe[:, :lanes]], axis=1),
                jnp.concatenate([vb[:, lanes:], kbe[:, lanes:]], axis=1)], axis=0).astype(BF16))
            kdts.append(jnp.transpose(jnp.concatenate([kd[:, :lanes], kd[:, lanes:]], axis=0)))

        xs = [_mm(x, split_halves(x, half_a)) for x in xs]
        for _ in range(n_sq - 2):
            rs = [_mm(jnp.concatenate([x, inv], axis=0), split_halves(x, half_a)) for x, inv in zip(xs, invs)]
            xs = [r[:c] for r in rs]
            invs = [inv + r[c:] for inv, r in zip(invs, rs)]
        invs = [inv + _mm(inv, split_halves(x, half_a)) for x, inv in zip(xs, invs)]

        sols = [_mm(split_halves(inv, half_a), rhs) for inv, rhs in zip(invs, rhss)]
        r3s = [_mm(jnp.concatenate([split_halves(intra, half_a), split_halves(kdt, half_a_sq)], axis=0), sol)
               for intra, kdt, sol in zip(intras, kdts, sols)]

        for (p, ci), r3, qe, cd in zip(items, r3s, qes, cds):
            r0 = ci * c
            for hx in range(2):
                hd = 2 * p + hx
                lo = hx * lanes
                s_x = states[hd]
                o0 = r3[hx * c:(hx + 1) * c, :lanes]
                iw = r3[hx * c:(hx + 1) * c, lanes:]
                n0 = r3[2 * c + hx * lanes:2 * c + (hx + 1) * lanes, :lanes]
                kw = r3[2 * c + hx * lanes:2 * c + (hx + 1) * lanes, lanes:]
                r4 = _mm(jnp.concatenate([qe[:, lo:lo + lanes] - iw, -kw], axis=0), s_x)
                o = r4[:c] + o0
                states[hd] = cd[:, lo:lo + lanes] * s_x + n0 + r4[c:]
                zt = z_ref[0, r0:r0 + c, hd * lanes:(hd + 1) * lanes].astype(F32)
                on = o * lax.rsqrt(jnp.mean(o * o, axis=-1, keepdims=True) + NORM_EPS) * nw
                o_ref[0, r0:r0 + c, hd * lanes:(hd + 1) * lanes] = (on * (zt * _sigmoid(zt))).astype(o_ref.dtype)

    chunks_per_group = GDN_GROUP_CHUNKS
    for c0 in range(0, ts // c, chunks_per_group):
        process([(p, ci) for ci in range(c0, c0 + chunks_per_group) for p in range(DN_PAIRS)])
    for hd in range(DN_HEADS):
        state_ref[hd] = states[hd]


def _gdn(qkv, z, beta, gc, gr, norm_w, ts):
    b, s, w = qkv.shape
    return pl.pallas_call(
        _gdn_kernel,
        grid=(b, s // ts),
        in_specs=[
            pl.BlockSpec((1, ts, w), lambda bi, i: (bi, i, 0)),
            pl.BlockSpec((1, ts, DN_W), lambda bi, i: (bi, i, 0)),
            pl.BlockSpec((1, ts, V7X_LANES), lambda bi, i: (bi, i, 0)),
            pl.BlockSpec((1, ts, V7X_LANES), lambda bi, i: (bi, i, 0)),
            pl.BlockSpec((1, 8, ts), lambda bi, i: (bi, 0, i)),
            pl.BlockSpec((1, V7X_LANES), lambda bi, i: (0, 0)),
        ],
        out_specs=pl.BlockSpec((1, ts, DN_W), lambda bi, i: (bi, i, 0)),
        out_shape=jax.ShapeDtypeStruct((b, s, DN_W), BF16),
        scratch_shapes=[pltpu.VMEM((DN_HEADS, DN_DK, DN_DK), F32)],
        compiler_params=_cparams(2),
        name="gdn",
    )(qkv, z, beta, gc, gr, norm_w.astype(F32)[None, :])


SWA_Q_SCALE = SWA_DH ** -0.5 * LOG2_E


def _proj_b_kernel(h_ref, w_ref, cos_ref, sin_ref, qt_ref, qm_ref):
    lanes = V7X_LANES
    xb = h_ref[0].astype(BF16)
    cos_t = cos_ref[0]
    sin_t = sin_ref[0]
    for c0 in range(0, SWA_Q_W, 2 * lanes):
        acc = jnp.dot(xb, w_ref[:, c0:c0 + 2 * lanes], preferred_element_type=F32)
        for half in range(2):
            q = _rope_pairs(acc[:, half * lanes:(half + 1) * lanes], cos_t, sin_t) * SWA_Q_SCALE
            qt_ref[0, c0 // lanes + half] = jnp.transpose(q).astype(BF16)
    qm_ref[0] = jnp.dot(xb, w_ref[:, SWA_Q_W:], preferred_element_type=F32).astype(qm_ref.dtype)


def _proj_b(h, w, layer, cos_t, sin_t, tm):
    b, s, d = h.shape
    lanes = V7X_LANES
    cur = lambda bi, i: (bi, i, 0)
    return pl.pallas_call(
        _proj_b_kernel,
        grid=(b, s // tm),
        in_specs=[
            pl.BlockSpec((1, tm, d), cur),
            pl.BlockSpec((None,) + w.shape[1:], lambda bi, i: (layer, 0, 0)),
            pl.BlockSpec((1, tm, lanes), cur),
            pl.BlockSpec((1, tm, lanes), cur),
        ],
        out_specs=[
            pl.BlockSpec((1, SWA_PAIRS, lanes, tm), lambda bi, i: (bi, 0, 0, i)),
            pl.BlockSpec((1, tm, MEM_W), cur),
        ],
        out_shape=[
            jax.ShapeDtypeStruct((b, SWA_PAIRS, lanes, s), BF16),
            jax.ShapeDtypeStruct((b, s, MEM_W), BF16),
        ],
        compiler_params=_cparams(2),
        name="proj_b",
    )(h, w, cos_t, sin_t)


def _shared_kv_kernel(h_ref, w_ref, cos_ref, sin_ref, kvar_ref, vt_ref):
    lanes = V7X_LANES
    acc = jnp.dot(h_ref[0].astype(BF16), w_ref[...], preferred_element_type=F32)
    k_rot = _rope_pairs(acc[:, :lanes], cos_ref[0], sin_ref[0])
    v = acc[:, lanes:]
    lane = lax.broadcasted_iota(jnp.int32, k_rot.shape, 1)
    low = lane < SWA_DH
    k_sw = pltpu.roll(k_rot, SWA_DH, axis=1)
    kvar_ref[0, 0] = jnp.where(low, k_rot, 0.0).astype(BF16)
    kvar_ref[0, 1] = jnp.where(low, 0.0, k_sw).astype(BF16)
    kvar_ref[0, 2] = jnp.where(low, k_sw, 0.0).astype(BF16)
    kvar_ref[0, 3] = jnp.where(low, 0.0, k_rot).astype(BF16)
    v_sw = pltpu.roll(v, SWA_DH, axis=1)
    ones_hi = jnp.where(lane == SWA_DH, 1.0, 0.0)
    ones_lo = jnp.where(lane == 0, 1.0, 0.0)
    vt_ref[0, 0] = jnp.transpose(jnp.where(low, v, ones_hi)).astype(BF16)
    vt_ref[0, 1] = jnp.transpose(jnp.where(low, ones_lo, v_sw)).astype(BF16)
    vt_ref[0, 2] = jnp.transpose(jnp.where(low, v_sw, ones_hi)).astype(BF16)
    vt_ref[0, 3] = jnp.transpose(jnp.where(low, ones_lo, v)).astype(BF16)


def _shared_kv(h, w, cos_t, sin_t, tm):
    b, s, d = h.shape
    lanes = V7X_LANES
    cur = lambda bi, i: (bi, i, 0)
    return pl.pallas_call(
        _shared_kv_kernel,
        grid=(b, s // tm),
        in_specs=[
            pl.BlockSpec((1, tm, d), cur),
            pl.BlockSpec(w.shape, lambda bi, i: (0, 0)),
            pl.BlockSpec((1, tm, lanes), cur),
            pl.BlockSpec((1, tm, lanes), cur),
        ],
        out_specs=[
            pl.BlockSpec((1, 4, tm, lanes), lambda bi, i: (bi, 0, i, 0)),
            pl.BlockSpec((1, 4, lanes, tm), lambda bi, i: (bi, 0, 0, i)),
        ],
        out_shape=[
            jax.ShapeDtypeStruct((b, 4, s, lanes), BF16),
            jax.ShapeDtypeStruct((b, 4, lanes, s), BF16),
        ],
        compiler_params=_cparams(2),
        name="shared_kv",
    )(h, w, cos_t, sin_t)


def _swa_kernel(sink_ref, q_ref, kv_ref, kvh_ref, vt_ref, vth_ref, o_ref):
    tq = q_ref.shape[3]
    lanes = V7X_LANES
    w = WINDOW
    first_key = jnp.where(pl.program_id(1) == 0, w, 0)
    sum_row = (SWA_DH, 0)

    def key_rows(var, j):
        if j == 0:
            return jnp.concatenate([kvh_ref[0, var], kv_ref[0, var, 0:w, :]], axis=0)
        return kv_ref[0, var, (j - 1) * w:(j + 1) * w, :]

    def value_cols(var, j):
        if j == 0:
            return jnp.concatenate([vth_ref[0, var], vt_ref[0, var, :, 0:w]], axis=1)
        return vt_ref[0, var, :, (j - 1) * w:(j + 1) * w]

    cols3 = SWA_PAIRS_PER_KV * w
    kj = lax.broadcasted_iota(jnp.int32, (2 * w, cols3), 0)
    qi = lax.broadcasted_iota(jnp.int32, (2 * w, cols3), 1) & (w - 1)
    dist = qi + w - kj
    band = (dist >= 0) & (dist < w)
    neg_inf = jnp.float32(-jnp.inf)
    bias_band = jnp.where(band, 0.0, neg_inf)
    bias_first = jnp.where(band & (kj >= first_key), 0.0, neg_inf)
    head_slot = lax.broadcasted_iota(jnp.int32, (1, cols3), 1) // w
    row_low = lax.broadcasted_iota(jnp.int32, (lanes, cols3), 0) < SWA_DH
    combos = [(g, e) for g in range(SWA_KV_HEADS) for e in range(2)]
    group_pairs = [[SWA_PAIRS_PER_KV * g + t for t in range(SWA_PAIRS_PER_KV)] for g in range(SWA_KV_HEADS)]

    def scores(j):
        r0 = j * w
        out = []
        for g, e in combos:
            qs = jnp.concatenate([q_ref[0, p, :, r0:r0 + w] for p in group_pairs[g]], axis=1)
            out.append(jnp.dot(key_rows(2 * g + e, j), qs, preferred_element_type=F32))
        return out

    def finish(j, ss):
        r0 = j * w
        bias = bias_first if j == 0 else bias_band
        pes, sink_terms = [], []
        for (g, e), s in zip(combos, ss):
            s = s + bias
            hs = [2 * p + e for p in group_pairs[g]]
            sink = jnp.where(head_slot == 0, sink_ref[hs[0]],
                             jnp.where(head_slot == 1, sink_ref[hs[1]], sink_ref[hs[2]])) * LOG2_E
            m = jnp.maximum(jnp.max(s, axis=0, keepdims=True), sink)
            pes.append(jnp.exp2(s - m).astype(BF16))
            sink_terms.append(jnp.exp2(sink - m))
        pvs = [jnp.dot(value_cols(2 * g + e, j), pe, preferred_element_type=F32)
               for (g, e), pe in zip(combos, pes)]
        for g in range(SWA_KV_HEADS):
            normed = []
            for e in range(2):
                pv = pvs[2 * g + e]
                den = pv[sum_row[e]:sum_row[e] + 1, :] + sink_terms[2 * g + e]
                normed.append(pv * (1.0 / den))
            acc = jnp.where(row_low, normed[0], normed[1])
            for t, p in enumerate(group_pairs[g]):
                o_ref[0, r0:r0 + w, p * lanes:(p + 1) * lanes] = (
                    jnp.transpose(acc[:, t * w:(t + 1) * w]).astype(o_ref.dtype))

    n_blocks = tq // w
    ss = scores(0)
    for j in range(n_blocks):
        nxt = scores(j + 1) if j + 1 < n_blocks else None
        finish(j, ss)
        ss = nxt


def _swa(q_t, kvar, vt, sinks, tq):
    b, _, lanes, s = q_t.shape
    assert s % tq == 0 and tq % WINDOW == 0
    hb = tq // WINDOW
    prev = lambda i: jnp.maximum(i * hb - 1, 0)
    return pl.pallas_call(
        _swa_kernel,
        grid=(b, s // tq),
        in_specs=[
            pl.BlockSpec(memory_space=pltpu.SMEM),
            pl.BlockSpec((1, SWA_PAIRS, lanes, tq), lambda bi, i: (bi, 0, 0, i)),
            pl.BlockSpec((1, 4, tq, lanes), lambda bi, i: (bi, 0, i, 0)),
            pl.BlockSpec((1, 4, WINDOW, lanes), lambda bi, i: (bi, 0, prev(i), 0)),
            pl.BlockSpec((1, 4, lanes, tq), lambda bi, i: (bi, 0, 0, i)),
            pl.BlockSpec((1, 4, lanes, WINDOW), lambda bi, i: (bi, 0, 0, prev(i))),
        ],
        out_specs=pl.BlockSpec((1, tq, SWA_Q_W), lambda bi, i: (bi, i, 0)),
        out_shape=jax.ShapeDtypeStruct((b, s, SWA_Q_W), BF16),
        compiler_params=_cparams(2),
        name="swa",
    )(sinks.astype(F32), q_t, kvar, kvar, vt, vt)


def _mix_out_kernel(o_ref, qm_ref, mk_ref, mvt_ref, wo_ref, h_ref, g_ref, b_ref, out_ref):
    mk = mk_ref[0].astype(F32)
    mvt = mvt_ref[0]
    lane_head = lax.broadcasted_iota(jnp.int32, mk.shape, 1) // MEM_DH
    heads = range(MEM_HEADS)
    khs = [jnp.where(lane_head == hd, mk, 0.0).astype(BF16) for hd in heads]
    ones = jnp.ones((BF16_SUBLANES, mvt.shape[1]), BF16)
    vhs = [jnp.concatenate([mvt[hd * MEM_DH:(hd + 1) * MEM_DH, :], ones], axis=0) for hd in heads]
    dn = o_ref.shape[2]
    qm = qm_ref[0].astype(F32) * (MEM_DH ** -0.5 * LOG2_E)
    qm_t = jnp.transpose(qm).astype(BF16)
    ss = [jnp.dot(kh, qm_t, preferred_element_type=F32) for kh in khs]
    mix = jnp.dot(o_ref[0], wo_ref[0:dn, :], preferred_element_type=F32)
    pes = [jnp.exp2(s - jnp.max(s, axis=0, keepdims=True)).astype(BF16) for s in ss]
    parts = []
    for vh, pe in zip(vhs, pes):
        pv = jnp.dot(vh, pe, preferred_element_type=F32)
        parts.append(pv[:MEM_DH] * (1.0 / pv[MEM_DH:MEM_DH + 1]))
    mo = jnp.transpose(jnp.concatenate(parts, axis=0))
    mix = mix + jnp.dot(mo.astype(BF16), wo_ref[dn:, :], preferred_element_type=F32)
    out_ref[0] = _layer_norm(DN_ALPHA * h_ref[0] + mix, g_ref[...], b_ref[...])


def _mix_out(o, qm, mk, mvt, wo, layer, h, ln_g, ln_b, tm):
    b, s, d = h.shape
    m_tok = mk.shape[1]
    cur = lambda bi, i: (bi, i, 0)
    per_b = lambda bi, i: (bi, 0, 0)
    const = lambda bi, i: (0, 0)
    return pl.pallas_call(
        _mix_out_kernel,
        grid=(b, s // tm),
        in_specs=[
            pl.BlockSpec((1, tm, o.shape[2]), cur),
            pl.BlockSpec((1, tm, MEM_W), cur),
            pl.BlockSpec((1, m_tok, MEM_W), per_b),
            pl.BlockSpec((1, MEM_W, m_tok), per_b),
            pl.BlockSpec((None,) + wo.shape[1:], lambda bi, i: (layer, 0, 0)),
            pl.BlockSpec((1, tm, d), cur),
            pl.BlockSpec((1, d), const),
            pl.BlockSpec((1, d), const),
        ],
        out_specs=pl.BlockSpec((1, tm, d), cur),
        out_shape=jax.ShapeDtypeStruct((b, s, d), F32),
        compiler_params=_cparams(2),
        name="mix_out",
    )(o, qm, mk, mvt, wo, h, ln_g[None, :], ln_b[None, :])


def _mlp_kernel(x_ref, up_ref, dn_ref, g_ref, b_ref, o_ref, *, hid_chunk):
    x = x_ref[...]
    xb = x.astype(BF16)
    hidden = up_ref.shape[1]
    acc = jnp.zeros(x.shape, F32)
    for j in range(0, hidden, hid_chunk):
        hid = jnp.dot(xb, up_ref[:, j:j + hid_chunk], preferred_element_type=F32)
        hid = jnp.square(jnp.maximum(hid, 0.0))
        acc = acc + jnp.dot(hid.astype(BF16), dn_ref[j:j + hid_chunk, :], preferred_element_type=F32)
    o_ref[...] = _layer_norm(DN_ALPHA * x + acc, g_ref[...], b_ref[...])


def _mlp(x, w_up, w_down, layer, ln_g, ln_b, tm):
    m, d = x.shape
    hidden = w_up.shape[2]
    return pl.pallas_call(
        functools.partial(_mlp_kernel, hid_chunk=512),
        grid=(m // tm,),
        in_specs=[
            pl.BlockSpec((tm, d), lambda i: (i, 0)),
            pl.BlockSpec((None, d, hidden), lambda i: (layer, 0, 0), pipeline_mode=pl.Buffered(1)),
            pl.BlockSpec((None, hidden, d), lambda i: (layer, 0, 0), pipeline_mode=pl.Buffered(1)),
            pl.BlockSpec((1, d), lambda i: (0, 0)),
            pl.BlockSpec((1, d), lambda i: (0, 0)),
        ],
        out_specs=pl.BlockSpec((tm, d), lambda i: (i, 0)),
        out_shape=jax.ShapeDtypeStruct((m, d), F32),
        compiler_params=_cparams(1),
        name="mlp",
    )(x, w_up, w_down, ln_g[None, :], ln_b[None, :])


def _a_weights_kernel(w_ref, o_ref):
    lanes = V7X_LANES
    ab0 = A_QM_COL
    o_ref[:, 0:A_QM_COL] = w_ref[:, 0:A_QM_COL].astype(BF16)
    o_ref[:, A_QM_COL:A_AB_COL] = w_ref[:, ab0 + 2 * DN_HEADS:ab0 + 2 * DN_HEADS + MEM_W].astype(BF16)
    first = w_ref[:, ab0:ab0 + lanes]
    lane = lax.broadcasted_iota(jnp.int32, first.shape, 1)
    a = jnp.where(lane < DN_HEADS, first, 0.0)
    b = jnp.where(lane < DN_HEADS, pltpu.roll(first, lanes - DN_HEADS, axis=1), 0.0)
    o_ref[:, A_AB_COL:A_AB_COL + lanes] = a.astype(BF16)
    o_ref[:, A_AB_COL + lanes:] = b.astype(BF16)


def _a_proj_weights(w_in_all, layer):
    _, k, n = w_in_all.shape
    assert n == A_QM_COL + 2 * DN_HEADS + MEM_W
    tk = 256
    return pl.pallas_call(
        _a_weights_kernel,
        grid=(k // tk,),
        in_specs=[pl.BlockSpec((None, tk, n), lambda i: (layer, i, 0))],
        out_specs=pl.BlockSpec((tk, A_W_COLS), lambda i: (i, 0)),
        out_shape=jax.ShapeDtypeStruct((k, A_W_COLS), BF16),
        compiler_params=_cparams(1),
        name="a_weights",
    )(w_in_all)


def _tile(n, pref):
    t = min(n, pref)
    assert n % t == 0
    return t


def kernel(x, mem, positions, a_w_in, a_conv_w, a_A_log, a_dt_bias, a_norm_w, b_w_in, b_sinks,
           w_kv_shared, mem_w_kv, w_o, mlp_w_up, mlp_w_down, ln_g, ln_b):
    b, s, d = x.shape
    t = b * s
    m_tok = mem.shape[1]
    tm = _tile(s, ROW_TILE)
    mem2 = mem.reshape(b * m_tok, d)
    h = x
    cos_t = sin_t = kvar = vt = None
    w_o_b, b_w_in_b = w_o.astype(BF16), b_w_in.astype(BF16)
    w_up_b, w_down_b = mlp_w_up.astype(BF16), mlp_w_down.astype(BF16)
    for layer in range(DEPTH):
        mk, mv = _proj(mem2, mem_w_kv[layer].astype(BF16), ((0, MEM_W), (MEM_W, MEM_W)), (BF16, BF16),
                       b * m_tok)
        mk = mk.reshape(b, m_tok, MEM_W)
        mvt = jnp.swapaxes(mv.reshape(b, m_tok, MEM_W), 1, 2)
        if layer < N_A:
            qkv, z, qm, beta, gc, gr = _proj_a(h, _a_proj_weights(a_w_in, layer), a_conv_w[layer],
                                               a_A_log[layer], a_dt_bias[layer], tm)
            ts = _tile(s, GDN_TILE)
            o = _gdn(qkv, z, beta, gc, gr, a_norm_w[layer], ts)
        else:
            j = layer - N_A
            q_t, qm = _proj_b(h, b_w_in_b, j, cos_t, sin_t, tm)
            o = _swa(q_t, kvar, vt, b_sinks[j], _tile(s, SWA_TILE))
        h = _mix_out(o, qm, mk, mvt, w_o_b, layer, h, ln_g[layer, 0], ln_b[layer, 0], tm)
        h = _mlp(h.reshape(t, d), w_up_b, w_down_b, layer, ln_g[layer, 1], ln_b[layer, 1], tm).reshape(b, s, d)
        if layer == N_A - 1:
            cos_t, sin_t = _rope_tables(positions)
            kvar, vt = _shared_kv(h, w_kv_shared.astype(BF16), cos_t, sin_t, tm)
    return h
```

```python
import functools

import jax
import jax.numpy as jnp
from jax import lax
from jax.experimental import pallas as pl
from jax.experimental.pallas import tpu as pltpu

F32 = jnp.float32
BF16 = jnp.bfloat16

V7X_LANES = 128
V7X_VMEM_LIMIT_BYTES = 56 * 1024 * 1024
BF16_SUBLANES = 16

DEPTH = 4
N_A = DEPTH // 2
MEM_HEADS = 4
MEM_DH = 64
MEM_W = MEM_HEADS * MEM_DH
DN_DK = 128
DN_HEADS = 6
DN_PAIRS = DN_HEADS // 2
DN_W = DN_HEADS * DN_DK
CONV_WIDTH = 4
CHUNK = 64
SWA_DH = 64
SWA_HEADS = 12
SWA_KV_HEADS = 2
SWA_Q_W = SWA_HEADS * SWA_DH
SWA_PAIRS = SWA_Q_W // V7X_LANES
SWA_PAIRS_PER_KV = SWA_PAIRS // SWA_KV_HEADS
WINDOW = 128
ROPE_THETA = 10000.0
LN_EPS = 1e-5
NORM_EPS = 1e-6
DN_ALPHA = (2.0 * DEPTH) ** 0.25
LOG2_E = 1.4426950408889634

ROW_TILE = 1024
GDN_TILE = 512
SWA_TILE = 1024


def _cparams(n_grid_axes):
    return pltpu.CompilerParams(
        dimension_semantics=("arbitrary",) * n_grid_axes,
        vmem_limit_bytes=V7X_VMEM_LIMIT_BYTES)


def _mm(a, b):
    return jnp.dot(a.astype(BF16), b.astype(BF16), preferred_element_type=F32)


def _mm_nt(a, b):
    return lax.dot_general(a.astype(BF16), b.astype(BF16), (((1,), (1,)), ((), ())),
                           preferred_element_type=F32)


def _sigmoid(x):
    return 1.0 / (1.0 + jnp.exp(-x))


def _layer_norm(y, g, b):
    mu = jnp.mean(y, axis=-1, keepdims=True)
    d = y - mu
    var = jnp.mean(d * d, axis=-1, keepdims=True)
    return d * lax.rsqrt(var + LN_EPS) * g + b


def _proj_kernel(x_ref, w_ref, *out_refs, segs, col_chunk):
    xb = x_ref[...].astype(BF16)
    for o_ref, (start, width) in zip(out_refs, segs):
        for c0 in range(0, width, col_chunk):
            c1 = min(c0 + col_chunk, width)
            acc = jnp.dot(xb, w_ref[:, start + c0:start + c1], preferred_element_type=F32)
            o_ref[:, c0:c1] = acc.astype(o_ref.dtype)


def _proj(x, w, segs, dtypes, tm):
    m, k = x.shape
    n = w.shape[1]
    assert m % tm == 0
    return pl.pallas_call(
        functools.partial(_proj_kernel, segs=segs, col_chunk=512),
        grid=(m // tm,),
        in_specs=[pl.BlockSpec((tm, k), lambda i: (i, 0)),
                  pl.BlockSpec((k, n), lambda i: (0, 0))],
        out_specs=[pl.BlockSpec((tm, wd), lambda i: (i, 0)) for _, wd in segs],
        out_shape=[jax.ShapeDtypeStruct((m, wd), dt) for (_, wd), dt in zip(segs, dtypes)],
        compiler_params=_cparams(1),
        name="proj",
    )(x, w)


ROPE_HALF = SWA_DH // 2
ROPE_PACK = V7X_LANES // ROPE_HALF


def _expand_rope(dense, dst_ref, signed):
    n = dense.shape[0]
    lane = lax.broadcasted_iota(jnp.int32, dense.shape, 1)
    group = lane // ROPE_HALF
    sign = jnp.where((group & 1) == 0, -1.0, 1.0)
    for k in range(ROPE_PACK):
        x = jnp.where(group == k, dense, 0.0)
        x = x + pltpu.roll(x, ROPE_HALF, axis=1)
        x = x + pltpu.roll(x, 2 * ROPE_HALF, axis=1)
        if signed:
            x = x * sign
        dst_ref[pl.ds(k, n, stride=ROPE_PACK), :] = x


def _rope_table_kernel(pos_ref, inv_ref, cos_ref, sin_ref):
    ang = pos_ref[...].astype(F32) * inv_ref[...]
    _expand_rope(jnp.cos(ang), cos_ref, False)
    _expand_rope(jnp.sin(ang), sin_ref, True)


def _rope_tables(positions):
    b, s = positions.shape
    inv_freq = ROPE_THETA ** (-jnp.arange(0, SWA_DH, 2, dtype=F32) / SWA_DH)
    pos_rep = jnp.broadcast_to(positions[..., None], (b, s, ROPE_HALF)).reshape(b * s // ROPE_PACK, V7X_LANES)
    inv_rep = jnp.tile(inv_freq, ROPE_PACK)[None, :]
    rows = pos_rep.shape[0]
    tr = min(rows, 1024)
    assert rows % tr == 0
    cos_t, sin_t = pl.pallas_call(
        _rope_table_kernel,
        grid=(rows // tr,),
        in_specs=[pl.BlockSpec((tr, V7X_LANES), lambda i: (i, 0)),
                  pl.BlockSpec((1, V7X_LANES), lambda i: (0, 0))],
        out_specs=[pl.BlockSpec((tr * ROPE_PACK, V7X_LANES), lambda i: (i, 0))] * 2,
        out_shape=[jax.ShapeDtypeStruct((rows * ROPE_PACK, V7X_LANES), F32)] * 2,
        compiler_params=_cparams(1),
        name="rope_tables",
    )(pos_rep, inv_rep)
    return cos_t.reshape(b, s, V7X_LANES), sin_t.reshape(b, s, V7X_LANES)


def _rope_pairs(x, cos_t, sin_t):
    lane = lax.broadcasted_iota(jnp.int32, x.shape, 1)
    first_half = (lane & (SWA_DH - 1)) < (SWA_DH // 2)
    swapped = jnp.where(first_half,
                        pltpu.roll(x, V7X_LANES - SWA_DH // 2, axis=1),
                        pltpu.roll(x, SWA_DH // 2, axis=1))
    return x * cos_t + swapped * sin_t


A_QM_COL = 4 * DN_W
A_AB_COL = A_QM_COL + MEM_W
A_W_COLS = A_AB_COL + 2 * V7X_LANES
CONV_PAD = 8
CONV_COLS = 2 * V7X_LANES


def _proj_a_kernel(h_ref, w_ref, cw_ref, alog_ref, dtb_ref,
                   qkv_ref, z_ref, qm_ref, beta_ref, gc_ref, gr_ref, ext_ref, carry_ref):
    tm = h_ref.shape[1]
    lanes = V7X_LANES

    @pl.when(pl.program_id(1) == 0)
    def _():
        carry_ref[...] = jnp.zeros_like(carry_ref)

    xb = h_ref[0].astype(BF16)

    def matmul(c0, width):
        return jnp.dot(xb, w_ref[:, c0:c0 + width], preferred_element_type=F32)

    def conv_block(c0):
        for half in range(CONV_COLS // lanes):
            cc = c0 + half * lanes
            y = cw_ref[CONV_WIDTH - 1:CONV_WIDTH, cc:cc + lanes] * ext_ref[CONV_PAD:CONV_PAD + tm, cc:cc + lanes]
            for j in range(CONV_WIDTH - 1):
                r0 = CONV_PAD - (CONV_WIDTH - 1) + j
                y = y + cw_ref[j:j + 1, cc:cc + lanes] * ext_ref[r0:r0 + tm, cc:cc + lanes]
            y = y * _sigmoid(y)
            if cc < 2 * DN_W:
                y = y * lax.rsqrt(jnp.sum(y * y, axis=-1, keepdims=True) + NORM_EPS)
            qkv_ref[0, :, cc:cc + lanes] = y.astype(qkv_ref.dtype)

    starts = list(range(0, 3 * DN_W, CONV_COLS))
    ext_ref[0:CONV_PAD, :] = carry_ref[...]
    for c0 in starts:
        acc = matmul(c0, CONV_COLS)
        ext_ref[CONV_PAD:CONV_PAD + tm, c0:c0 + CONV_COLS] = acc
        carry_ref[:, c0:c0 + CONV_COLS] = acc[tm - CONV_PAD:tm, :]

    for c0 in range(0, DN_W, CONV_COLS):
        z_ref[0, :, c0:c0 + CONV_COLS] = matmul(3 * DN_W + c0, CONV_COLS).astype(z_ref.dtype)
    qm_ref[0] = matmul(A_QM_COL, MEM_W).astype(qm_ref.dtype)

    ab = matmul(A_AB_COL, 2 * lanes)
    for c0 in starts:
        conv_block(c0)
    xg = ab[:, :lanes] + dtb_ref[...]
    softplus = jnp.maximum(xg, 0.0) + jnp.log1p(jnp.exp(-jnp.abs(xg)))
    g = -jnp.exp(alog_ref[...]) * softplus
    beta_ref[0] = _sigmoid(ab[:, lanes:])

    row = lax.broadcasted_iota(jnp.int32, g.shape, 0) & (CHUNK - 1)
    gc = g
    step = 1
    while step < CHUNK:
        gc = gc + jnp.where(row >= step, pltpu.roll(gc, step, axis=0), 0.0)
        step *= 2
    gc_ref[0] = gc

    gct = jnp.transpose(gc)[0:8, :]
    plus = pltpu.roll(pltpu.roll(gct, CHUNK, axis=1), 7, axis=0)
    minus = pltpu.roll(pltpu.roll(gct, tm - CHUNK, axis=1), 1, axis=0)
    lane = lax.broadcasted_iota(jnp.int32, gct.shape, 1)
    low = (lane & (lanes - 1)) < CHUNK
    even = (lax.broadcasted_iota(jnp.int32, gct.shape, 0) & 1) == 0
    gr_ref[0] = jnp.where(even, jnp.where(low, gct, plus), jnp.where(low, minus, gct))


def _proj_a(h, w, conv_w, a_log, dt_bias, tm):
    b, s, d = h.shape
    assert s % tm == 0 and tm % V7X_LANES == 0 and w.shape == (d, A_W_COLS)
    pad = V7X_LANES - DN_HEADS
    alog = jnp.pad(a_log.astype(F32), (0, pad))[None, :]
    dtb = jnp.pad(dt_bias.astype(F32), (0, pad))[None, :]
    cur = lambda bi, i: (bi, i, 0)
    const = lambda bi, i: (0, 0)
    lanes = V7X_LANES
    return pl.pallas_call(
        _proj_a_kernel,
        grid=(b, s // tm),
        in_specs=[
            pl.BlockSpec((1, tm, d), cur),
            pl.BlockSpec((d, A_W_COLS), const),
            pl.BlockSpec((CONV_WIDTH, 3 * DN_W), const),
            pl.BlockSpec((1, lanes), const),
            pl.BlockSpec((1, lanes), const),
        ],
        out_specs=[
            pl.BlockSpec((1, tm, 3 * DN_W), cur),
            pl.BlockSpec((1, tm, DN_W), cur),
            pl.BlockSpec((1, tm, MEM_W), cur),
            pl.BlockSpec((1, tm, lanes), cur),
            pl.BlockSpec((1, tm, lanes), cur),
            pl.BlockSpec((1, 8, tm), lambda bi, i: (bi, 0, i)),
        ],
        out_shape=[
            jax.ShapeDtypeStruct((b, s, 3 * DN_W), BF16),
            jax.ShapeDtypeStruct((b, s, DN_W), BF16),
            jax.ShapeDtypeStruct((b, s, MEM_W), BF16),
            jax.ShapeDtypeStruct((b, s, lanes), F32),
            jax.ShapeDtypeStruct((b, s, lanes), F32),
            jax.ShapeDtypeStruct((b, 8, s), F32),
        ],
        scratch_shapes=[pltpu.VMEM((CONV_PAD + tm, 3 * DN_W), F32),
                        pltpu.VMEM((CONV_PAD, 3 * DN_W), F32)],
        compiler_params=_cparams(2),
        name="proj_a",
    )(h, w, conv_w.astype(F32), alog, dtb)


def _gdn_kernel(qkv_ref, z_ref, beta_ref, gc_ref, gr_ref, nw_ref, o_ref, state_ref):
    ts = qkv_ref.shape[1]
    c = CHUNK
    lanes = V7X_LANES

    @pl.when(pl.program_id(1) == 0)
    def _():
        state_ref[...] = jnp.zeros_like(state_ref)

    lane2 = lax.broadcasted_iota(jnp.int32, (c, 2 * lanes), 1)
    head_a2 = lane2 < lanes
    lane1 = lax.broadcasted_iota(jnp.int32, (c, lanes), 1)
    half_a = lane1 < c
    col = lane1 & (c - 1)
    row = lax.broadcasted_iota(jnp.int32, (c, lanes), 0)
    incl = row >= col
    strict = row > col
    eye = jnp.where(row == col, 1.0, 0.0)
    lane_sq = lax.broadcasted_iota(jnp.int32, (lanes, lanes), 1)
    half_a_sq = lane_sq < c
    nw = nw_ref[...]
    q_scale = DN_DK ** -0.5

    def split_halves(x, mask):
        return jnp.concatenate([jnp.where(mask, x, 0.0), jnp.where(mask, 0.0, x)], axis=0)

    n_sq = c.bit_length() - 1
    states = [state_ref[hd] for hd in range(DN_HEADS)]

    def process(items):
        xs, invs, intras, rhss, kdts, qes, cds = [], [], [], [], [], [], []
        for p, ci in items:
            ha, hb = 2 * p, 2 * p + 1
            r0 = ci * c
            qp = qkv_ref[0, r0:r0 + c, 2 * lanes * p:2 * lanes * (p + 1)].astype(F32)
            kp = qkv_ref[0, r0:r0 + c, DN_W + 2 * lanes * p:DN_W + 2 * lanes * (p + 1)].astype(F32)
            vp = qkv_ref[0, r0:r0 + c, 2 * DN_W + 2 * lanes * p:2 * DN_W + 2 * lanes * (p + 1)].astype(F32)
            beta_t = beta_ref[0, r0:r0 + c, :]
            gc_t = gc_ref[0, r0:r0 + c, :]
            b_a, b_b = beta_t[:, ha:ha + 1], beta_t[:, hb:hb + 1]
            g_a, g_b = gc_t[:, ha:ha + 1], gc_t[:, hb:hb + 1]
            beta2 = jnp.where(head_a2, b_a, b_b)
            g2 = jnp.where(head_a2, g_a, g_b)
            g1 = jnp.where(half_a, g_a, g_b)
            g_row = gr_ref[0, 2 * p + (ci % 2):2 * p + (ci % 2) + 1, lanes * (ci // 2):lanes * (ci // 2 + 1)]
            g_last = g2[c - 1:c, :]
            eg = jnp.exp(g2)
            kb = kp * beta2
            vb = vp * beta2
            kbe = kb * eg
            qs = qp * q_scale
            kd = kp * jnp.exp(g_last - g2)
            qes.append(qs * eg)
            cds.append(jnp.exp(g_last))

            r1 = _mm_nt(jnp.concatenate([kb, qs], axis=0), split_halves(kp, head_a2))
            decay = jnp.where(incl, jnp.exp(g1 - g_row), 0.0)
            x = jnp.where(strict, -(r1[:c] * decay), 0.0)
            xs.append(x)
            invs.append(eye + x)
            intras.append(r1[c:] * decay)
            rhss.append(jnp.concatenate([
                jnp.concatenate([vb[:, :lanes], kbe[:, :lanes]], axis=1),
                jnp.concatenate([vb[:, lanes:], kbe[:, lanes:]], axis=1)], axis=0).astype(BF16))
            kdts.append(jnp.transpose(jnp.concatenate([kd[:, :lanes], kd[:, lanes:]], axis=0)))

        xs = [_mm(x, split_halves(x, half_a)) for x in xs]
        for _ in range(n_sq - 2):
            rs = [_mm(jnp.concatenate([x, inv], axis=0), split_halves(x, half_a)) for x, inv in zip(xs, invs)]
            xs = [r[:c] for r in rs]
            invs = [inv + r[c:] for inv, r in zip(invs, rs)]
        invs = [inv + _mm(inv, split_halves(x, half_a)) for x, inv in zip(xs, invs)]

        sols = [_mm(split_halves(inv, half_a), rhs) for inv, rhs in zip(invs, rhss)]
        r3s = [_mm(jnp.concatenate([split_halves(intra, half_a), split_halves(kdt, half_a_sq)], axis=0), sol)
               for intra, kdt, sol in zip(intras, kdts, sols)]

        for (p, ci), r3, qe, cd in zip(items, r3s, qes, cds):
            r0 = ci * c
            for hx in range(2):
                hd = 2 * p + hx
                lo = hx * lanes
                s_x = states[hd]
                o0 = r3[hx * c:(hx + 1) * c, :lanes]
                iw = r3[hx * c:(hx + 1) * c, lanes:]
                n0 = r3[2 * c + hx * lanes:2 * c + (hx + 1) * lanes, :lanes]
                kw = r3[2 * c + hx * lanes:2 * c + (hx + 1) * lanes, lanes:]
                r4 = _mm(jnp.concatenate([qe[:, lo:lo + lanes] - iw, -kw], axis=0), s_x)
                o = r4[:c] + o0
                states[hd] = cd[:, lo:lo + lanes] * s_x + n0 + r4[c:]
                zt = z_ref[0, r0:r0 + c, hd * lanes:(hd + 1) * lanes].astype(F32)
                on = o * lax.rsqrt(jnp.mean(o * o, axis=-1, keepdims=True) + NORM_EPS) * nw
                o_ref[0, r0:r0 + c, hd * lanes:(hd + 1) * lanes] = (on * (zt * _sigmoid(zt))).astype(o_ref.dtype)

    process([(p, ci) for ci in range(ts // c) for p in range(DN_PAIRS)])
    for hd in range(DN_HEADS):
        state_ref[hd] = states[hd]


def _gdn(qkv, z, beta, gc, gr, norm_w, ts):
    b, s, w = qkv.shape
    return pl.pallas_call(
        _gdn_kernel,
        grid=(b, s // ts),
        in_specs=[
            pl.BlockSpec((1, ts, w), lambda bi, i: (bi, i, 0)),
            pl.BlockSpec((1, ts, DN_W), lambda bi, i: (bi, i, 0)),
            pl.BlockSpec((1, ts, V7X_LANES), lambda bi, i: (bi, i, 0)),
            pl.BlockSpec((1, ts, V7X_LANES), lambda bi, i: (bi, i, 0)),
            pl.BlockSpec((1, 8, ts), lambda bi, i: (bi, 0, i)),
            pl.BlockSpec((1, V7X_LANES), lambda bi, i: (0, 0)),
        ],
        out_specs=pl.BlockSpec((1, ts, DN_W), lambda bi, i: (bi, i, 0)),
        out_shape=jax.ShapeDtypeStruct((b, s, DN_W), BF16),
        scratch_shapes=[pltpu.VMEM((DN_HEADS, DN_DK, DN_DK), F32)],
        compiler_params=_cparams(2),
        name="gdn",
    )(qkv, z, beta, gc, gr, norm_w.astype(F32)[None, :])


SWA_Q_SCALE = SWA_DH ** -0.5 * LOG2_E


def _proj_b_kernel(h_ref, w_ref, cos_ref, sin_ref, qt_ref, qm_ref):
    lanes = V7X_LANES
    xb = h_ref[0].astype(BF16)
    cos_t = cos_ref[0]
    sin_t = sin_ref[0]
    for c0 in range(0, SWA_Q_W, 2 * lanes):
        acc = jnp.dot(xb, w_ref[:, c0:c0 + 2 * lanes], preferred_element_type=F32)
        for half in range(2):
            q = _rope_pairs(acc[:, half * lanes:(half + 1) * lanes], cos_t, sin_t) * SWA_Q_SCALE
            qt_ref[0, c0 // lanes + half] = jnp.transpose(q).astype(BF16)
    qm_ref[0] = jnp.dot(xb, w_ref[:, SWA_Q_W:], preferred_element_type=F32).astype(qm_ref.dtype)


def _proj_b(h, w, layer, cos_t, sin_t, tm):
    b, s, d = h.shape
    lanes = V7X_LANES
    cur = lambda bi, i: (bi, i, 0)
    return pl.pallas_call(
        _proj_b_kernel,
        grid=(b, s // tm),
        in_specs=[
            pl.BlockSpec((1, tm, d), cur),
            pl.BlockSpec((None,) + w.shape[1:], lambda bi, i: (layer, 0, 0)),
            pl.BlockSpec((1, tm, lanes), cur),
            pl.BlockSpec((1, tm, lanes), cur),
        ],
        out_specs=[
            pl.BlockSpec((1, SWA_PAIRS, lanes, tm), lambda bi, i: (bi, 0, 0, i)),
            pl.BlockSpec((1, tm, MEM_W), cur),
        ],
        out_shape=[
            jax.ShapeDtypeStruct((b, SWA_PAIRS, lanes, s), BF16),
            jax.ShapeDtypeStruct((b, s, MEM_W), BF16),
        ],
        compiler_params=_cparams(2),
        name="proj_b",
    )(h, w, cos_t, sin_t)


def _shared_kv_kernel(h_ref, w_ref, cos_ref, sin_ref, kvar_ref, vt_ref):
    lanes = V7X_LANES
    acc = jnp.dot(h_ref[0].astype(BF16), w_ref[...], preferred_element_type=F32)
    k_rot = _rope_pairs(acc[:, :lanes], cos_ref[0], sin_ref[0])
    v = acc[:, lanes:]
    lane = lax.broadcasted_iota(jnp.int32, k_rot.shape, 1)
    low = lane < SWA_DH
    k_sw = pltpu.roll(k_rot, SWA_DH, axis=1)
    kvar_ref[0, 0] = jnp.where(low, k_rot, 0.0).astype(BF16)
    kvar_ref[0, 1] = jnp.where(low, 0.0, k_sw).astype(BF16)
    kvar_ref[0, 2] = jnp.where(low, k_sw, 0.0).astype(BF16)
    kvar_ref[0, 3] = jnp.where(low, 0.0, k_rot).astype(BF16)
    v_sw = pltpu.roll(v, SWA_DH, axis=1)
    ones_hi = jnp.where(lane == SWA_DH, 1.0, 0.0)
    ones_lo = jnp.where(lane == 0, 1.0, 0.0)
    vt_ref[0, 0] = jnp.transpose(jnp.where(low, v, ones_hi)).astype(BF16)
    vt_ref[0, 1] = jnp.transpose(jnp.where(low, ones_lo, v_sw)).astype(BF16)
    vt_ref[0, 2] = jnp.transpose(jnp.where(low, v_sw, ones_hi)).astype(BF16)
    vt_ref[0, 3] = jnp.transpose(jnp.where(low, ones_lo, v)).astype(BF16)


def _shared_kv(h, w, cos_t, sin_t, tm):
    b, s, d = h.shape
    lanes = V7X_LANES
    cur = lambda bi, i: (bi, i, 0)
    return pl.pallas_call(
        _shared_kv_kernel,
        grid=(b, s // tm),
        in_specs=[
            pl.BlockSpec((1, tm, d), cur),
            pl.BlockSpec(w.shape, lambda bi, i: (0, 0)),
            pl.BlockSpec((1, tm, lanes), cur),
            pl.BlockSpec((1, tm, lanes), cur),
        ],
        out_specs=[
            pl.BlockSpec((1, 4, tm, lanes), lambda bi, i: (bi, 0, i, 0)),
            pl.BlockSpec((1, 4, lanes, tm), lambda bi, i: (bi, 0, 0, i)),
        ],
        out_shape=[
            jax.ShapeDtypeStruct((b, 4, s, lanes), BF16),
            jax.ShapeDtypeStruct((b, 4, lanes, s), BF16),
        ],
        compiler_params=_cparams(2),
        name="shared_kv",
    )(h, w, cos_t, sin_t)


def _swa_kernel(sink_ref, q_ref, kv_ref, kvh_ref, vt_ref, vth_ref, o_ref):
    tq = q_ref.shape[3]
    lanes = V7X_LANES
    w = WINDOW
    first_key = jnp.where(pl.program_id(1) == 0, w, 0)
    sum_row = (SWA_DH, 0)

    def key_rows(var, j):
        if j == 0:
            return jnp.concatenate([kvh_ref[0, var], kv_ref[0, var, 0:w, :]], axis=0)
        return kv_ref[0, var, (j - 1) * w:(j + 1) * w, :]

    def value_cols(var, j):
        if j == 0:
            return jnp.concatenate([vth_ref[0, var], vt_ref[0, var, :, 0:w]], axis=1)
        return vt_ref[0, var, :, (j - 1) * w:(j + 1) * w]

    cols3 = SWA_PAIRS_PER_KV * w
    kj = lax.broadcasted_iota(jnp.int32, (2 * w, cols3), 0)
    qi = lax.broadcasted_iota(jnp.int32, (2 * w, cols3), 1) & (w - 1)
    dist = qi + w - kj
    band = (dist >= 0) & (dist < w)
    neg_inf = jnp.float32(-jnp.inf)
    bias_band = jnp.where(band, 0.0, neg_inf)
    bias_first = jnp.where(band & (kj >= first_key), 0.0, neg_inf)
    head_slot = lax.broadcasted_iota(jnp.int32, (1, cols3), 1) // w
    row_low = lax.broadcasted_iota(jnp.int32, (lanes, cols3), 0) < SWA_DH
    combos = [(g, e) for g in range(SWA_KV_HEADS) for e in range(2)]
    group_pairs = [[SWA_PAIRS_PER_KV * g + t for t in range(SWA_PAIRS_PER_KV)] for g in range(SWA_KV_HEADS)]

    def scores(j):
        r0 = j * w
        out = []
        for g, e in combos:
            qs = jnp.concatenate([q_ref[0, p, :, r0:r0 + w] for p in group_pairs[g]], axis=1)
            out.append(jnp.dot(key_rows(2 * g + e, j), qs, preferred_element_type=F32))
        return out

    def finish(j, ss):
        r0 = j * w
        bias = bias_first if j == 0 else bias_band
        pes, sink_terms = [], []
        for (g, e), s in zip(combos, ss):
            s = s + bias
            hs = [2 * p + e for p in group_pairs[g]]
            sink = jnp.where(head_slot == 0, sink_ref[hs[0]],
                             jnp.where(head_slot == 1, sink_ref[hs[1]], sink_ref[hs[2]])) * LOG2_E
            m = jnp.maximum(jnp.max(s, axis=0, keepdims=True), sink)
            pes.append(jnp.exp2(s - m).astype(BF16))
            sink_terms.append(jnp.exp2(sink - m))
        pvs = [jnp.dot(value_cols(2 * g + e, j), pe, preferred_element_type=F32)
               for (g, e), pe in zip(combos, pes)]
        for g in range(SWA_KV_HEADS):
            normed = []
            for e in range(2):
                pv = pvs[2 * g + e]
                den = pv[sum_row[e]:sum_row[e] + 1, :] + sink_terms[2 * g + e]
                normed.append(pv * (1.0 / den))
            acc = jnp.where(row_low, normed[0], normed[1])
            for t, p in enumerate(group_pairs[g]):
                o_ref[0, r0:r0 + w, p * lanes:(p + 1) * lanes] = (
                    jnp.transpose(acc[:, t * w:(t + 1) * w]).astype(o_ref.dtype))

    n_blocks = tq // w
    ss = scores(0)
    for j in range(n_blocks):
        nxt = scores(j + 1) if j + 1 < n_blocks else None
        finish(j, ss)
        ss = nxt


def _swa(q_t, kvar, vt, sinks, tq):
    b, _, lanes, s = q_t.shape
    assert s % tq == 0 and tq % WINDOW == 0
    hb = tq // WINDOW
    prev = lambda i: jnp.maximum(i * hb - 1, 0)
    return pl.pallas_call(
        _swa_kernel,
        grid=(b, s // tq),
        in_specs=[
            pl.BlockSpec(memory_space=pltpu.SMEM),
            pl.BlockSpec((1, SWA_PAIRS, lanes, tq), lambda bi, i: (bi, 0, 0, i)),
            pl.BlockSpec((1, 4, tq, lanes), lambda bi, i: (bi, 0, i, 0)),
            pl.BlockSpec((1, 4, WINDOW, lanes), lambda bi, i: (bi, 0, prev(i), 0)),
            pl.BlockSpec((1, 4, lanes, tq), lambda bi, i: (bi, 0, 0, i)),
            pl.BlockSpec((1, 4, lanes, WINDOW), lambda bi, i: (bi, 0, 0, prev(i))),
        ],
        out_specs=pl.BlockSpec((1, tq, SWA_Q_W), lambda bi, i: (bi, i, 0)),
        out_shape=jax.ShapeDtypeStruct((b, s, SWA_Q_W), BF16),
        compiler_params=_cparams(2),
        name="swa",
    )(sinks.astype(F32), q_t, kvar, kvar, vt, vt)


def _mix_out_kernel(o_ref, qm_ref, mk_ref, mvt_ref, wo_ref, h_ref, g_ref, b_ref, out_ref):
    mk = mk_ref[0].astype(F32)
    mvt = mvt_ref[0]
    lane_head = lax.broadcasted_iota(jnp.int32, mk.shape, 1) // MEM_DH
    heads = range(MEM_HEADS)
    khs = [jnp.where(lane_head == hd, mk, 0.0).astype(BF16) for hd in heads]
    ones = jnp.ones((BF16_SUBLANES, mvt.shape[1]), BF16)
    vhs = [jnp.concatenate([mvt[hd * MEM_DH:(hd + 1) * MEM_DH, :], ones], axis=0) for hd in heads]
    dn = o_ref.shape[2]
    qm = qm_ref[0].astype(F32) * (MEM_DH ** -0.5 * LOG2_E)
    qm_t = jnp.transpose(qm).astype(BF16)
    ss = [jnp.dot(kh, qm_t, preferred_element_type=F32) for kh in khs]
    half = qm_t.shape[1] // 2
    mix_a = jnp.dot(o_ref[0, 0:half, :], wo_ref[0:dn, :], preferred_element_type=F32)
    pes = [jnp.exp2(s - jnp.max(s, axis=0, keepdims=True)).astype(BF16) for s in ss]
    parts = []
    for vh, pe in zip(vhs, pes):
        pv = jnp.dot(vh, pe, preferred_element_type=F32)
        parts.append(pv[:MEM_DH] * (1.0 / pv[MEM_DH:MEM_DH + 1]))
    mix_b = jnp.dot(o_ref[0, half:, :], wo_ref[0:dn, :], preferred_element_type=F32)
    mix = jnp.concatenate([mix_a, mix_b], axis=0)
    mo = jnp.transpose(jnp.concatenate(parts, axis=0))
    mix = mix + jnp.dot(mo.astype(BF16), wo_ref[dn:, :], preferred_element_type=F32)
    out_ref[0] = _layer_norm(DN_ALPHA * h_ref[0] + mix, g_ref[...], b_ref[...])


def _mix_out(o, qm, mk, mvt, wo, layer, h, ln_g, ln_b, tm):
    b, s, d = h.shape
    m_tok = mk.shape[1]
    cur = lambda bi, i: (bi, i, 0)
    per_b = lambda bi, i: (bi, 0, 0)
    const = lambda bi, i: (0, 0)
    return pl.pallas_call(
        _mix_out_kernel,
        grid=(b, s // tm),
        in_specs=[
            pl.BlockSpec((1, tm, o.shape[2]), cur),
            pl.BlockSpec((1, tm, MEM_W), cur),
            pl.BlockSpec((1, m_tok, MEM_W), per_b),
            pl.BlockSpec((1, MEM_W, m_tok), per_b),
            pl.BlockSpec((None,) + wo.shape[1:], lambda bi, i: (layer, 0, 0)),
            pl.BlockSpec((1, tm, d), cur),
            pl.BlockSpec((1, d), const),
            pl.BlockSpec((1, d), const),
        ],
        out_specs=pl.BlockSpec((1, tm, d), cur),
        out_shape=jax.ShapeDtypeStruct((b, s, d), F32),
        compiler_params=_cparams(2),
        name="mix_out",
    )(o, qm, mk, mvt, wo, h, ln_g[None, :], ln_b[None, :])


def _mlp_kernel(x_ref, up_ref, dn_ref, g_ref, b_ref, o_ref, *, hid_chunk):
    x = x_ref[...]
    xb = x.astype(BF16)
    hidden = up_ref.shape[1]
    acc = jnp.zeros(x.shape, F32)
    for j in range(0, hidden, hid_chunk):
        hid = jnp.dot(xb, up_ref[:, j:j + hid_chunk], preferred_element_type=F32)
        hid = jnp.square(jnp.maximum(hid, 0.0))
        acc = acc + jnp.dot(hid.astype(BF16), dn_ref[j:j + hid_chunk, :], preferred_element_type=F32)
    o_ref[...] = _layer_norm(DN_ALPHA * x + acc, g_ref[...], b_ref[...])


def _mlp(x, w_up, w_down, layer, ln_g, ln_b, tm):
    m, d = x.shape
    hidden = w_up.shape[2]
    return pl.pallas_call(
        functools.partial(_mlp_kernel, hid_chunk=512),
        grid=(m // tm,),
        in_specs=[
            pl.BlockSpec((tm, d), lambda i: (i, 0)),
            pl.BlockSpec((None, d, hidden), lambda i: (layer, 0, 0), pipeline_mode=pl.Buffered(1)),
            pl.BlockSpec((None, hidden, d), lambda i: (layer, 0, 0), pipeline_mode=pl.Buffered(1)),
            pl.BlockSpec((1, d), lambda i: (0, 0)),
            pl.BlockSpec((1, d), lambda i: (0, 0)),
        ],
        out_specs=pl.BlockSpec((tm, d), lambda i: (i, 0)),
        out_shape=jax.ShapeDtypeStruct((m, d), F32),
        compiler_params=_cparams(1),
        name="mlp",
    )(x, w_up, w_down, ln_g[None, :], ln_b[None, :])


def _a_weights_kernel(w_ref, o_ref):
    lanes = V7X_LANES
    ab0 = A_QM_COL
    o_ref[:, 0:A_QM_COL] = w_ref[:, 0:A_QM_COL].astype(BF16)
    o_ref[:, A_QM_COL:A_AB_COL] = w_ref[:, ab0 + 2 * DN_HEADS:ab0 + 2 * DN_HEADS + MEM_W].astype(BF16)
    first = w_ref[:, ab0:ab0 + lanes]
    lane = lax.broadcasted_iota(jnp.int32, first.shape, 1)
    a = jnp.where(lane < DN_HEADS, first, 0.0)
    b = jnp.where(lane < DN_HEADS, pltpu.roll(first, lanes - DN_HEADS, axis=1), 0.0)
    o_ref[:, A_AB_COL:A_AB_COL + lanes] = a.astype(BF16)
    o_ref[:, A_AB_COL + lanes:] = b.astype(BF16)


def _a_proj_weights(w_in_all, layer):
    _, k, n = w_in_all.shape
    assert n == A_QM_COL + 2 * DN_HEADS + MEM_W
    tk = 256
    return pl.pallas_call(
        _a_weights_kernel,
        grid=(k // tk,),
        in_specs=[pl.BlockSpec((None, tk, n), lambda i: (layer, i, 0))],
        out_specs=pl.BlockSpec((tk, A_W_COLS), lambda i: (i, 0)),
        out_shape=jax.ShapeDtypeStruct((k, A_W_COLS), BF16),
        compiler_params=_cparams(1),
        name="a_weights",
    )(w_in_all)


def _tile(n, pref):
    t = min(n, pref)
    assert n % t == 0
    return t


def kernel(x, mem, positions, a_w_in, a_conv_w, a_A_log, a_dt_bias, a_norm_w, b_w_in, b_sinks,
           w_kv_shared, mem_w_kv, w_o, mlp_w_up, mlp_w_down, ln_g, ln_b):
    b, s, d = x.shape
    t = b * s
    m_tok = mem.shape[1]
    tm = _tile(s, ROW_TILE)
    mem2 = mem.reshape(b * m_tok, d)
    h = x
    cos_t = sin_t = kvar = vt = None
    w_o_b, b_w_in_b = w_o.astype(BF16), b_w_in.astype(BF16)
    w_up_b, w_down_b = mlp_w_up.astype(BF16), mlp_w_down.astype(BF16)
    for layer in range(DEPTH):
        mk, mv = _proj(mem2, mem_w_kv[layer].astype(BF16), ((0, MEM_W), (MEM_W, MEM_W)), (BF16, BF16),
                       b * m_tok)
        mk = mk.reshape(b, m_tok, MEM_W)
        mvt = jnp.swapaxes(mv.reshape(b, m_tok, MEM_W), 1, 2)
        if layer < N_A:
            qkv, z, qm, beta, gc, gr = _proj_a(h, _a_proj_weights(a_w_in, layer), a_conv_w[layer],
                                               a_A_log[layer], a_dt_bias[layer], tm)
            ts = _tile(s, GDN_TILE)
            o = _gdn(qkv, z, beta, gc, gr, a_norm_w[layer], ts)
        else:
            j = layer - N_A
            q_t, qm = _proj_b(h, b_w_in_b, j, cos_t, sin_t, tm)
            o = _swa(q_t, kvar, vt, b_sinks[j], _tile(s, SWA_TILE))
        h = _mix_out(o, qm, mk, mvt, w_o_b, layer, h, ln_g[layer, 0], ln_b[layer, 0], tm)
        h = _mlp(h.reshape(t, d), w_up_b, w_down_b, layer, ln_g[layer, 1], ln_b[layer, 1], tm).reshape(b, s, d)
        if layer == N_A - 1:
            cos_t, sin_t = _rope_tables(positions)
            kvar, vt = _shared_kv(h, w_kv_shared.astype(BF16), cos_t, sin_t, tm)
    return h
```

```python
import functools

import jax
import jax.numpy as jnp
from jax import lax
from jax.experimental import pallas as pl
from jax.experimental.pallas import tpu as pltpu

F32 = jnp.float32
BF16 = jnp.bfloat16

V7X_LANES = 128
V7X_VMEM_LIMIT_BYTES = 56 * 1024 * 1024
BF16_SUBLANES = 16

DEPTH = 4
N_A = DEPTH // 2
MEM_HEADS = 4
MEM_DH = 64
MEM_W = MEM_HEADS * MEM_DH
DN_DK = 128
DN_HEADS = 6
DN_PAIRS = DN_HEADS // 2
DN_W = DN_HEADS * DN_DK
CONV_WIDTH = 4
CHUNK = 64
SWA_DH = 64
SWA_HEADS = 12
SWA_KV_HEADS = 2
SWA_Q_W = SWA_HEADS * SWA_DH
SWA_PAIRS = SWA_Q_W // V7X_LANES
SWA_PAIRS_PER_KV = SWA_PAIRS // SWA_KV_HEADS
WINDOW = 128
ROPE_THETA = 10000.0
LN_EPS = 1e-5
NORM_EPS = 1e-6
DN_ALPHA = (2.0 * DEPTH) ** 0.25
LOG2_E = 1.4426950408889634

ROW_TILE = 1024
GDN_TILE = 1024
SWA_TILE = 1024


def _cparams(n_grid_axes):
    return pltpu.CompilerParams(
        dimension_semantics=("arbitrary",) * n_grid_axes,
        vmem_limit_bytes=V7X_VMEM_LIMIT_BYTES)


def _mm(a, b):
    return jnp.dot(a.astype(BF16), b.astype(BF16), preferred_element_type=F32)


def _mm_nt(a, b):
    return lax.dot_general(a.astype(BF16), b.astype(BF16), (((1,), (1,)), ((), ())),
                           preferred_element_type=F32)


def _sigmoid(x):
    return 1.0 / (1.0 + jnp.exp(-x))


def _layer_norm(y, g, b):
    mu = jnp.mean(y, axis=-1, keepdims=True)
    d = y - mu
    var = jnp.mean(d * d, axis=-1, keepdims=True)
    return d * lax.rsqrt(var + LN_EPS) * g + b


def _proj_kernel(x_ref, w_ref, *out_refs, segs, col_chunk):
    xb = x_ref[...].astype(BF16)
    for o_ref, (start, width) in zip(out_refs, segs):
        for c0 in range(0, width, col_chunk):
            c1 = min(c0 + col_chunk, width)
            acc = jnp.dot(xb, w_ref[:, start + c0:start + c1], preferred_element_type=F32)
            o_ref[:, c0:c1] = acc.astype(o_ref.dtype)


def _proj(x, w, segs, dtypes, tm):
    m, k = x.shape
    n = w.shape[1]
    assert m % tm == 0
    return pl.pallas_call(
        functools.partial(_proj_kernel, segs=segs, col_chunk=512),
        grid=(m // tm,),
        in_specs=[pl.BlockSpec((tm, k), lambda i: (i, 0)),
                  pl.BlockSpec((k, n), lambda i: (0, 0))],
        out_specs=[pl.BlockSpec((tm, wd), lambda i: (i, 0)) for _, wd in segs],
        out_shape=[jax.ShapeDtypeStruct((m, wd), dt) for (_, wd), dt in zip(segs, dtypes)],
        compiler_params=_cparams(1),
        name="proj",
    )(x, w)


ROPE_HALF = SWA_DH // 2
ROPE_PACK = V7X_LANES // ROPE_HALF


def _expand_rope(dense, dst_ref, signed):
    n = dense.shape[0]
    lane = lax.broadcasted_iota(jnp.int32, dense.shape, 1)
    group = lane // ROPE_HALF
    sign = jnp.where((group & 1) == 0, -1.0, 1.0)
    for k in range(ROPE_PACK):
        x = jnp.where(group == k, dense, 0.0)
        x = x + pltpu.roll(x, ROPE_HALF, axis=1)
        x = x + pltpu.roll(x, 2 * ROPE_HALF, axis=1)
        if signed:
            x = x * sign
        dst_ref[pl.ds(k, n, stride=ROPE_PACK), :] = x


def _rope_table_kernel(pos_ref, inv_ref, cos_ref, sin_ref):
    ang = pos_ref[...].astype(F32) * inv_ref[...]
    _expand_rope(jnp.cos(ang), cos_ref, False)
    _expand_rope(jnp.sin(ang), sin_ref, True)


def _rope_tables(positions):
    b, s = positions.shape
    inv_freq = ROPE_THETA ** (-jnp.arange(0, SWA_DH, 2, dtype=F32) / SWA_DH)
    pos_rep = jnp.broadcast_to(positions[..., None], (b, s, ROPE_HALF)).reshape(b * s // ROPE_PACK, V7X_LANES)
    inv_rep = jnp.tile(inv_freq, ROPE_PACK)[None, :]
    rows = pos_rep.shape[0]
    tr = min(rows, 1024)
    assert rows % tr == 0
    cos_t, sin_t = pl.pallas_call(
        _rope_table_kernel,
        grid=(rows // tr,),
        in_specs=[pl.BlockSpec((tr, V7X_LANES), lambda i: (i, 0)),
                  pl.BlockSpec((1, V7X_LANES), lambda i: (0, 0))],
        out_specs=[pl.BlockSpec((tr * ROPE_PACK, V7X_LANES), lambda i: (i, 0))] * 2,
        out_shape=[jax.ShapeDtypeStruct((rows * ROPE_PACK, V7X_LANES), F32)] * 2,
        compiler_params=_cparams(1),
        name="rope_tables",
    )(pos_rep, inv_rep)
    return cos_t.reshape(b, s, V7X_LANES), sin_t.reshape(b, s, V7X_LANES)


def _rope_pairs(x, cos_t, sin_t):
    lane = lax.broadcasted_iota(jnp.int32, x.shape, 1)
    first_half = (lane & (SWA_DH - 1)) < (SWA_DH // 2)
    swapped = jnp.where(first_half,
                        pltpu.roll(x, V7X_LANES - SWA_DH // 2, axis=1),
                        pltpu.roll(x, SWA_DH // 2, axis=1))
    return x * cos_t + swapped * sin_t


A_QM_COL = 4 * DN_W
A_AB_COL = A_QM_COL + MEM_W
A_W_COLS = A_AB_COL + 2 * V7X_LANES
CONV_PAD = 8
CONV_COLS = 2 * V7X_LANES


def _proj_a_kernel(h_ref, w_ref, cw_ref, alog_ref, dtb_ref,
                   qkv_ref, z_ref, qm_ref, beta_ref, gc_ref, gr_ref, ext_ref, carry_ref):
    tm = h_ref.shape[1]
    lanes = V7X_LANES

    @pl.when(pl.program_id(1) == 0)
    def _():
        carry_ref[...] = jnp.zeros_like(carry_ref)

    xb = h_ref[0].astype(BF16)

    def matmul(c0, width):
        return jnp.dot(xb, w_ref[:, c0:c0 + width], preferred_element_type=F32)

    def conv_block(c0):
        for half in range(CONV_COLS // lanes):
            cc = c0 + half * lanes
            y = cw_ref[CONV_WIDTH - 1:CONV_WIDTH, cc:cc + lanes] * ext_ref[CONV_PAD:CONV_PAD + tm, cc:cc + lanes]
            for j in range(CONV_WIDTH - 1):
                r0 = CONV_PAD - (CONV_WIDTH - 1) + j
                y = y + cw_ref[j:j + 1, cc:cc + lanes] * ext_ref[r0:r0 + tm, cc:cc + lanes]
            y = y * _sigmoid(y)
            if cc < 2 * DN_W:
                y = y * lax.rsqrt(jnp.sum(y * y, axis=-1, keepdims=True) + NORM_EPS)
            qkv_ref[0, :, cc:cc + lanes] = y.astype(qkv_ref.dtype)

    starts = list(range(0, 3 * DN_W, CONV_COLS))
    ext_ref[0:CONV_PAD, :] = carry_ref[...]
    for c0 in starts:
        acc = matmul(c0, CONV_COLS)
        ext_ref[CONV_PAD:CONV_PAD + tm, c0:c0 + CONV_COLS] = acc
        carry_ref[:, c0:c0 + CONV_COLS] = acc[tm - CONV_PAD:tm, :]

    for c0 in range(0, DN_W, CONV_COLS):
        z_ref[0, :, c0:c0 + CONV_COLS] = matmul(3 * DN_W + c0, CONV_COLS).astype(z_ref.dtype)
    qm_ref[0] = matmul(A_QM_COL, MEM_W).astype(qm_ref.dtype)

    ab = matmul(A_AB_COL, 2 * lanes)
    for c0 in starts:
        conv_block(c0)
    xg = ab[:, :lanes] + dtb_ref[...]
    softplus = jnp.maximum(xg, 0.0) + jnp.log1p(jnp.exp(-jnp.abs(xg)))
    g = -jnp.exp(alog_ref[...]) * softplus
    beta_ref[0] = _sigmoid(ab[:, lanes:])

    row = lax.broadcasted_iota(jnp.int32, g.shape, 0) & (CHUNK - 1)
    gc = g
    step = 1
    while step < CHUNK:
        gc = gc + jnp.where(row >= step, pltpu.roll(gc, step, axis=0), 0.0)
        step *= 2
    gc_ref[0] = gc

    gct = jnp.transpose(gc)[0:8, :]
    plus = pltpu.roll(pltpu.roll(gct, CHUNK, axis=1), 7, axis=0)
    minus = pltpu.roll(pltpu.roll(gct, tm - CHUNK, axis=1), 1, axis=0)
    lane = lax.broadcasted_iota(jnp.int32, gct.shape, 1)
    low = (lane & (lanes - 1)) < CHUNK
    even = (lax.broadcasted_iota(jnp.int32, gct.shape, 0) & 1) == 0
    gr_ref[0] = jnp.where(even, jnp.where(low, gct, plus), jnp.where(low, minus, gct))


def _proj_a(h, w, conv_w, a_log, dt_bias, tm):
    b, s, d = h.shape
    assert s % tm == 0 and tm % V7X_LANES == 0 and w.shape == (d, A_W_COLS)
    pad = V7X_LANES - DN_HEADS
    alog = jnp.pad(a_log.astype(F32), (0, pad))[None, :]
    dtb = jnp.pad(dt_bias.astype(F32), (0, pad))[None, :]
    cur = lambda bi, i: (bi, i, 0)
    const = lambda bi, i: (0, 0)
    lanes = V7X_LANES
    return pl.pallas_call(
        _proj_a_kernel,
        grid=(b, s // tm),
        in_specs=[
            pl.BlockSpec((1, tm, d), cur),
            pl.BlockSpec((d, A_W_COLS), const),
            pl.BlockSpec((CONV_WIDTH, 3 * DN_W), const),
            pl.BlockSpec((1, lanes), const),
            pl.BlockSpec((1, lanes), const),
        ],
        out_specs=[
            pl.BlockSpec((1, tm, 3 * DN_W), cur),
            pl.BlockSpec((1, tm, DN_W), cur),
            pl.BlockSpec((1, tm, MEM_W), cur),
            pl.BlockSpec((1, tm, lanes), cur),
            pl.BlockSpec((1, tm, lanes), cur),
            pl.BlockSpec((1, 8, tm), lambda bi, i: (bi, 0, i)),
        ],
        out_shape=[
            jax.ShapeDtypeStruct((b, s, 3 * DN_W), BF16),
            jax.ShapeDtypeStruct((b, s, DN_W), BF16),
            jax.ShapeDtypeStruct((b, s, MEM_W), BF16),
            jax.ShapeDtypeStruct((b, s, lanes), F32),
            jax.ShapeDtypeStruct((b, s, lanes), F32),
            jax.ShapeDtypeStruct((b, 8, s), F32),
        ],
        scratch_shapes=[pltpu.VMEM((CONV_PAD + tm, 3 * DN_W), F32),
                        pltpu.VMEM((CONV_PAD, 3 * DN_W), F32)],
        compiler_params=_cparams(2),
        name="proj_a",
    )(h, w, conv_w.astype(F32), alog, dtb)


def _gdn_kernel(qkv_ref, z_ref, beta_ref, gc_ref, gr_ref, nw_ref, o_ref, state_ref):
    ts = qkv_ref.shape[1]
    c = CHUNK
    lanes = V7X_LANES

    @pl.when(pl.program_id(1) == 0)
    def _():
        state_ref[...] = jnp.zeros_like(state_ref)

    lane2 = lax.broadcasted_iota(jnp.int32, (c, 2 * lanes), 1)
    head_a2 = lane2 < lanes
    lane1 = lax.broadcasted_iota(jnp.int32, (c, lanes), 1)
    half_a = lane1 < c
    col = lane1 & (c - 1)
    row = lax.broadcasted_iota(jnp.int32, (c, lanes), 0)
    incl = row >= col
    strict = row > col
    eye = jnp.where(row == col, 1.0, 0.0)
    lane_sq = lax.broadcasted_iota(jnp.int32, (lanes, lanes), 1)
    half_a_sq = lane_sq < c
    nw = nw_ref[...]
    q_scale = DN_DK ** -0.5

    def split_halves(x, mask):
        return jnp.concatenate([jnp.where(mask, x, 0.0), jnp.where(mask, 0.0, x)], axis=0)

    n_sq = c.bit_length() - 1
    states = [state_ref[hd] for hd in range(DN_HEADS)]

    def process(items):
        xs, invs, intras, rhss, kdts, qes, cds = [], [], [], [], [], [], []
        for p, ci in items:
            ha, hb = 2 * p, 2 * p + 1
            r0 = ci * c
            qp = qkv_ref[0, r0:r0 + c, 2 * lanes * p:2 * lanes * (p + 1)].astype(F32)
            kp = qkv_ref[0, r0:r0 + c, DN_W + 2 * lanes * p:DN_W + 2 * lanes * (p + 1)].astype(F32)
            vp = qkv_ref[0, r0:r0 + c, 2 * DN_W + 2 * lanes * p:2 * DN_W + 2 * lanes * (p + 1)].astype(F32)
            beta_t = beta_ref[0, r0:r0 + c, :]
            gc_t = gc_ref[0, r0:r0 + c, :]
            b_a, b_b = beta_t[:, ha:ha + 1], beta_t[:, hb:hb + 1]
            g_a, g_b = gc_t[:, ha:ha + 1], gc_t[:, hb:hb + 1]
            beta2 = jnp.where(head_a2, b_a, b_b)
            g2 = jnp.where(head_a2, g_a, g_b)
            g1 = jnp.where(half_a, g_a, g_b)
            g_row = gr_ref[0, 2 * p + (ci % 2):2 * p + (ci % 2) + 1, lanes * (ci // 2):lanes * (ci // 2 + 1)]
            g_last = g2[c - 1:c, :]
            eg = jnp.exp(g2)
            kb = kp * beta2
            vb = vp * beta2
            kbe = kb * eg
            qs = qp * q_scale
            kd = kp * jnp.exp(g_last - g2)
            qes.append(qs * eg)
            cds.append(jnp.exp(g_last))

            r1 = _mm_nt(jnp.concatenate([kb, qs], axis=0), split_halves(kp, head_a2))
            decay = jnp.where(incl, jnp.exp(g1 - g_row), 0.0)
            x = jnp.where(strict, -(r1[:c] * decay), 0.0)
            xs.append(x)
            invs.append(eye + x)
            intras.append(r1[c:] * decay)
            rhss.append(jnp.concatenate([
                jnp.concatenate([vb[:, :lanes], kbe[:, :lanes]], axis=1),
                jnp.concatenate([vb[:, lanes:], kbe[:, lanes:]], axis=1)], axis=0).astype(BF16))
            kdts.append(jnp.transpose(jnp.concatenate([kd[:, :lanes], kd[:, lanes:]], axis=0)))

        xs = [_mm(x, split_halves(x, half_a)) for x in xs]
        for _ in range(n_sq - 2):
            rs = [_mm(jnp.concatenate([x, inv], axis=0), split_halves(x, half_a)) for x, inv in zip(xs, invs)]
            xs = [r[:c] for r in rs]
            invs = [inv + r[c:] for inv, r in zip(invs, rs)]
        invs = [inv + _mm(inv, split_halves(x, half_a)) for x, inv in zip(xs, invs)]

        sols = [_mm(split_halves(inv, half_a), rhs) for inv, rhs in zip(invs, rhss)]
        r3s = [_mm(jnp.concatenate([split_halves(intra, half_a), split_halves(kdt, half_a_sq)], axis=0), sol)
               for intra, kdt, sol in zip(intras, kdts, sols)]

        for (p, ci), r3, qe, cd in zip(items, r3s, qes, cds):
            r0 = ci * c
            for hx in range(2):
                hd = 2 * p + hx
                lo = hx * lanes
                s_x = states[hd]
                o0 = r3[hx * c:(hx + 1) * c, :lanes]
                iw = r3[hx * c:(hx + 1) * c, lanes:]
                n0 = r3[2 * c + hx * lanes:2 * c + (hx + 1) * lanes, :lanes]
                kw = r3[2 * c + hx * lanes:2 * c + (hx + 1) * lanes, lanes:]
                r4 = _mm(jnp.concatenate([qe[:, lo:lo + lanes] - iw, -kw], axis=0), s_x)
                o = r4[:c] + o0
                states[hd] = cd[:, lo:lo + lanes] * s_x + n0 + r4[c:]
                zt = z_ref[0, r0:r0 + c, hd * lanes:(hd + 1) * lanes].astype(F32)
                on = o * lax.rsqrt(jnp.mean(o * o, axis=-1, keepdims=True) + NORM_EPS) * nw
                o_ref[0, r0:r0 + c, hd * lanes:(hd + 1) * lanes] = (on * (zt * _sigmoid(zt))).astype(o_ref.dtype)

    process([(p, ci) for ci in range(ts // c) for p in range(DN_PAIRS)])
    for hd in range(DN_HEADS):
        state_ref[hd] = states[hd]


def _gdn(qkv, z, beta, gc, gr, norm_w, ts):
    b, s, w = qkv.shape
    return pl.pallas_call(
        _gdn_kernel,
        grid=(b, s // ts),
        in_specs=[
            pl.BlockSpec((1, ts, w), lambda bi, i: (bi, i, 0)),
            pl.BlockSpec((1, ts, DN_W), lambda bi, i: (bi, i, 0)),
            pl.BlockSpec((1, ts, V7X_LANES), lambda bi, i: (bi, i, 0)),
            pl.BlockSpec((1, ts, V7X_LANES), lambda bi, i: (bi, i, 0)),
            pl.BlockSpec((1, 8, ts), lambda bi, i: (bi, 0, i)),
            pl.BlockSpec((1, V7X_LANES), lambda bi, i: (0, 0)),
        ],
        out_specs=pl.BlockSpec((1, ts, DN_W), lambda bi, i: (bi, i, 0)),
        out_shape=jax.ShapeDtypeStruct((b, s, DN_W), BF16),
        scratch_shapes=[pltpu.VMEM((DN_HEADS, DN_DK, DN_DK), F32)],
        compiler_params=_cparams(2),
        name="gdn",
    )(qkv, z, beta, gc, gr, norm_w.astype(F32)[None, :])


SWA_Q_SCALE = SWA_DH ** -0.5 * LOG2_E


def _proj_b_kernel(h_ref, w_ref, cos_ref, sin_ref, qt_ref, qm_ref):
    lanes = V7X_LANES
    xb = h_ref[0].astype(BF16)
    cos_t = cos_ref[0]
    sin_t = sin_ref[0]
    for c0 in range(0, SWA_Q_W, 2 * lanes):
        acc = jnp.dot(xb, w_ref[:, c0:c0 + 2 * lanes], preferred_element_type=F32)
        for half in range(2):
            q = _rope_pairs(acc[:, half * lanes:(half + 1) * lanes], cos_t, sin_t) * SWA_Q_SCALE
            qt_ref[0, c0 // lanes + half] = jnp.transpose(q).astype(BF16)
    qm_ref[0] = jnp.dot(xb, w_ref[:, SWA_Q_W:], preferred_element_type=F32).astype(qm_ref.dtype)


def _proj_b(h, w, layer, cos_t, sin_t, tm):
    b, s, d = h.shape
    lanes = V7X_LANES
    cur = lambda bi, i: (bi, i, 0)
    return pl.pallas_call(
        _proj_b_kernel,
        grid=(b, s // tm),
        in_specs=[
            pl.BlockSpec((1, tm, d), cur),
            pl.BlockSpec((None,) + w.shape[1:], lambda bi, i: (layer, 0, 0)),
            pl.BlockSpec((1, tm, lanes), cur),
            pl.BlockSpec((1, tm, lanes), cur),
        ],
        out_specs=[
            pl.BlockSpec((1, SWA_PAIRS, lanes, tm), lambda bi, i: (bi, 0, 0, i)),
            pl.BlockSpec((1, tm, MEM_W), cur),
        ],
        out_shape=[
            jax.ShapeDtypeStruct((b, SWA_PAIRS, lanes, s), BF16),
            jax.ShapeDtypeStruct((b, s, MEM_W), BF16),
        ],
        compiler_params=_cparams(2),
        name="proj_b",
    )(h, w, cos_t, sin_t)


def _emit_shared_kv(h_tile, w_ref, cos_t, sin_t, kvar_ref, vt_ref):
    lanes = V7X_LANES
    acc = jnp.dot(h_tile.astype(BF16), w_ref[...], preferred_element_type=F32)
    k_rot = _rope_pairs(acc[:, :lanes], cos_t, sin_t)
    v = acc[:, lanes:]
    lane = lax.broadcasted_iota(jnp.int32, k_rot.shape, 1)
    low = lane < SWA_DH
    k_sw = pltpu.roll(k_rot, SWA_DH, axis=1)
    kvar_ref[0, 0] = jnp.where(low, k_rot, 0.0).astype(BF16)
    kvar_ref[0, 1] = jnp.where(low, 0.0, k_sw).astype(BF16)
    kvar_ref[0, 2] = jnp.where(low, k_sw, 0.0).astype(BF16)
    kvar_ref[0, 3] = jnp.where(low, 0.0, k_rot).astype(BF16)
    v_sw = pltpu.roll(v, SWA_DH, axis=1)
    ones_hi = jnp.where(lane == SWA_DH, 1.0, 0.0)
    ones_lo = jnp.where(lane == 0, 1.0, 0.0)
    vt_ref[0, 0] = jnp.transpose(jnp.where(low, v, ones_hi)).astype(BF16)
    vt_ref[0, 1] = jnp.transpose(jnp.where(low, ones_lo, v_sw)).astype(BF16)
    vt_ref[0, 2] = jnp.transpose(jnp.where(low, v_sw, ones_hi)).astype(BF16)
    vt_ref[0, 3] = jnp.transpose(jnp.where(low, ones_lo, v)).astype(BF16)


def _swa_kernel(sink_ref, q_ref, kv_ref, kvh_ref, vt_ref, vth_ref, o_ref):
    tq = q_ref.shape[3]
    lanes = V7X_LANES
    w = WINDOW
    first_key = jnp.where(pl.program_id(1) == 0, w, 0)
    sum_row = (SWA_DH, 0)

    def key_rows(var, j):
        if j == 0:
            return jnp.concatenate([kvh_ref[0, var], kv_ref[0, var, 0:w, :]], axis=0)
        return kv_ref[0, var, (j - 1) * w:(j + 1) * w, :]

    def value_cols(var, j):
        if j == 0:
            return jnp.concatenate([vth_ref[0, var], vt_ref[0, var, :, 0:w]], axis=1)
        return vt_ref[0, var, :, (j - 1) * w:(j + 1) * w]

    cols3 = SWA_PAIRS_PER_KV * w
    kj = lax.broadcasted_iota(jnp.int32, (2 * w, cols3), 0)
    qi = lax.broadcasted_iota(jnp.int32, (2 * w, cols3), 1) & (w - 1)
    dist = qi + w - kj
    band = (dist >= 0) & (dist < w)
    neg_inf = jnp.float32(-jnp.inf)
    bias_band = jnp.where(band, 0.0, neg_inf)
    bias_first = jnp.where(band & (kj >= first_key), 0.0, neg_inf)
    head_slot = lax.broadcasted_iota(jnp.int32, (1, cols3), 1) // w
    row_low = lax.broadcasted_iota(jnp.int32, (lanes, cols3), 0) < SWA_DH
    combos = [(g, e) for g in range(SWA_KV_HEADS) for e in range(2)]
    group_pairs = [[SWA_PAIRS_PER_KV * g + t for t in range(SWA_PAIRS_PER_KV)] for g in range(SWA_KV_HEADS)]

    def scores(j):
        r0 = j * w
        out = []
        for g, e in combos:
            qs = jnp.concatenate([q_ref[0, p, :, r0:r0 + w] for p in group_pairs[g]], axis=1)
            out.append(jnp.dot(key_rows(2 * g + e, j), qs, preferred_element_type=F32))
        return out

    def finish(j, ss):
        r0 = j * w
        bias = bias_first if j == 0 else bias_band
        pes, sink_terms = [], []
        for (g, e), s in zip(combos, ss):
            s = s + bias
            hs = [2 * p + e for p in group_pairs[g]]
            sink = jnp.where(head_slot == 0, sink_ref[hs[0]],
                             jnp.where(head_slot == 1, sink_ref[hs[1]], sink_ref[hs[2]])) * LOG2_E
            m = jnp.maximum(jnp.max(s, axis=0, keepdims=True), sink)
            pes.append(jnp.exp2(s - m).astype(BF16))
            sink_terms.append(jnp.exp2(sink - m))
        pvs = [jnp.dot(value_cols(2 * g + e, j), pe, preferred_element_type=F32)
               for (g, e), pe in zip(combos, pes)]
        for g in range(SWA_KV_HEADS):
            normed = []
            for e in range(2):
                pv = pvs[2 * g + e]
                den = pv[sum_row[e]:sum_row[e] + 1, :] + sink_terms[2 * g + e]
                normed.append(pv * (1.0 / den))
            acc = jnp.where(row_low, normed[0], normed[1])
            for t, p in enumerate(group_pairs[g]):
                o_ref[0, r0:r0 + w, p * lanes:(p + 1) * lanes] = (
                    jnp.transpose(acc[:, t * w:(t + 1) * w]).astype(o_ref.dtype))

    n_blocks = tq // w
    ss = scores(0)
    for j in range(n_blocks):
        nxt = scores(j + 1) if j + 1 < n_blocks else None
        finish(j, ss)
        ss = nxt


def _swa(q_t, kvar, vt, sinks, tq):
    b, _, lanes, s = q_t.shape
    assert s % tq == 0 and tq % WINDOW == 0
    hb = tq // WINDOW
    prev = lambda i: jnp.maximum(i * hb - 1, 0)
    return pl.pallas_call(
        _swa_kernel,
        grid=(b, s // tq),
        in_specs=[
            pl.BlockSpec(memory_space=pltpu.SMEM),
            pl.BlockSpec((1, SWA_PAIRS, lanes, tq), lambda bi, i: (bi, 0, 0, i)),
            pl.BlockSpec((1, 4, tq, lanes), lambda bi, i: (bi, 0, i, 0)),
            pl.BlockSpec((1, 4, WINDOW, lanes), lambda bi, i: (bi, 0, prev(i), 0)),
            pl.BlockSpec((1, 4, lanes, tq), lambda bi, i: (bi, 0, 0, i)),
            pl.BlockSpec((1, 4, lanes, WINDOW), lambda bi, i: (bi, 0, 0, prev(i))),
        ],
        out_specs=pl.BlockSpec((1, tq, SWA_Q_W), lambda bi, i: (bi, i, 0)),
        out_shape=jax.ShapeDtypeStruct((b, s, SWA_Q_W), BF16),
        compiler_params=_cparams(2),
        name="swa",
    )(sinks.astype(F32), q_t, kvar, kvar, vt, vt)


def _mix_out_kernel(o_ref, qm_ref, mk_ref, mvt_ref, wo_ref, h_ref, g_ref, b_ref, out_ref):
    mk = mk_ref[0].astype(F32)
    mvt = mvt_ref[0]
    lane_head = lax.broadcasted_iota(jnp.int32, mk.shape, 1) // MEM_DH
    heads = range(MEM_HEADS)
    khs = [jnp.where(lane_head == hd, mk, 0.0).astype(BF16) for hd in heads]
    ones = jnp.ones((BF16_SUBLANES, mvt.shape[1]), BF16)
    vhs = [jnp.concatenate([mvt[hd * MEM_DH:(hd + 1) * MEM_DH, :], ones], axis=0) for hd in heads]
    dn = o_ref.shape[2]
    qm = qm_ref[0].astype(F32) * (MEM_DH ** -0.5 * LOG2_E)
    qm_t = jnp.transpose(qm).astype(BF16)
    ss = [jnp.dot(kh, qm_t, preferred_element_type=F32) for kh in khs]
    half = qm_t.shape[1] // 2
    mix_a = jnp.dot(o_ref[0, 0:half, :], wo_ref[0:dn, :], preferred_element_type=F32)
    pes = [jnp.exp2(s - jnp.max(s, axis=0, keepdims=True)).astype(BF16) for s in ss]
    parts = []
    for vh, pe in zip(vhs, pes):
        pv = jnp.dot(vh, pe, preferred_element_type=F32)
        parts.append(pv[:MEM_DH] * (1.0 / pv[MEM_DH:MEM_DH + 1]))
    mix_b = jnp.dot(o_ref[0, half:, :], wo_ref[0:dn, :], preferred_element_type=F32)
    mix = jnp.concatenate([mix_a, mix_b], axis=0)
    mo = jnp.transpose(jnp.concatenate(parts, axis=0))
    mix = mix + jnp.dot(mo.astype(BF16), wo_ref[dn:, :], preferred_element_type=F32)
    out_ref[0] = _layer_norm(DN_ALPHA * h_ref[0] + mix, g_ref[...], b_ref[...])


def _mix_out(o, qm, mk, mvt, wo, layer, h, ln_g, ln_b, tm):
    b, s, d = h.shape
    m_tok = mk.shape[1]
    cur = lambda bi, i: (bi, i, 0)
    per_b = lambda bi, i: (bi, 0, 0)
    const = lambda bi, i: (0, 0)
    return pl.pallas_call(
        _mix_out_kernel,
        grid=(b, s // tm),
        in_specs=[
            pl.BlockSpec((1, tm, o.shape[2]), cur),
            pl.BlockSpec((1, tm, MEM_W), cur),
            pl.BlockSpec((1, m_tok, MEM_W), per_b),
            pl.BlockSpec((1, MEM_W, m_tok), per_b),
            pl.BlockSpec((None,) + wo.shape[1:], lambda bi, i: (layer, 0, 0)),
            pl.BlockSpec((1, tm, d), cur),
            pl.BlockSpec((1, d), const),
            pl.BlockSpec((1, d), const),
        ],
        out_specs=pl.BlockSpec((1, tm, d), cur),
        out_shape=jax.ShapeDtypeStruct((b, s, d), F32),
        compiler_params=_cparams(2),
        name="mix_out",
    )(o, qm, mk, mvt, wo, h, ln_g[None, :], ln_b[None, :])


def _mlp_kernel(x_ref, up_ref, dn_ref, g_ref, b_ref, *rest, hid_chunk, with_kv):
    o_ref = rest[3] if with_kv else rest[0]
    x = x_ref[...]
    xb = x.astype(BF16)
    hidden = up_ref.shape[1]
    acc = jnp.zeros(x.shape, F32)
    for j in range(0, hidden, hid_chunk):
        hid = jnp.dot(xb, up_ref[:, j:j + hid_chunk], preferred_element_type=F32)
        hid = jnp.square(jnp.maximum(hid, 0.0))
        acc = acc + jnp.dot(hid.astype(BF16), dn_ref[j:j + hid_chunk, :], preferred_element_type=F32)
    out = _layer_norm(DN_ALPHA * x + acc, g_ref[...], b_ref[...])
    o_ref[...] = out
    if with_kv:
        wkv_ref, cos_ref, sin_ref, _, kvar_ref, vt_ref = rest
        _emit_shared_kv(out, wkv_ref, cos_ref[0], sin_ref[0], kvar_ref, vt_ref)


def _mlp(x, w_up, w_down, layer, ln_g, ln_b, tm, shared_kv=None):
    m, d = x.shape
    hidden = w_up.shape[2]
    lanes = V7X_LANES
    in_specs = [
        pl.BlockSpec((tm, d), lambda i: (i, 0)),
        pl.BlockSpec((None, d, hidden), lambda i: (layer, 0, 0), pipeline_mode=pl.Buffered(1)),
        pl.BlockSpec((None, hidden, d), lambda i: (layer, 0, 0), pipeline_mode=pl.Buffered(1)),
        pl.BlockSpec((1, d), lambda i: (0, 0)),
        pl.BlockSpec((1, d), lambda i: (0, 0)),
    ]
    out_specs = [pl.BlockSpec((tm, d), lambda i: (i, 0))]
    out_shape = [jax.ShapeDtypeStruct((m, d), F32)]
    args = [x, w_up, w_down, ln_g[None, :], ln_b[None, :]]
    if shared_kv is not None:
        w_kv, cos_t, sin_t = shared_kv
        b, s, _ = cos_t.shape
        nb = s // tm
        assert b * s == m and s % tm == 0
        in_specs += [pl.BlockSpec(w_kv.shape, lambda i: (0, 0)),
                     pl.BlockSpec((1, tm, lanes), lambda i: (i // nb, i % nb, 0)),
                     pl.BlockSpec((1, tm, lanes), lambda i: (i // nb, i % nb, 0))]
        out_specs += [pl.BlockSpec((1, 4, tm, lanes), lambda i: (i // nb, 0, i % nb, 0)),
                      pl.BlockSpec((1, 4, lanes, tm), lambda i: (i // nb, 0, 0, i % nb))]
        out_shape += [jax.ShapeDtypeStruct((b, 4, s, lanes), BF16),
                      jax.ShapeDtypeStruct((b, 4, lanes, s), BF16)]
        args += [w_kv, cos_t, sin_t]
    return pl.pallas_call(
        functools.partial(_mlp_kernel, hid_chunk=512, with_kv=shared_kv is not None),
        grid=(m // tm,),
        in_specs=in_specs,
        out_specs=out_specs,
        out_shape=out_shape,
        compiler_params=_cparams(1),
        name="mlp",
    )(*args)


def _a_weights_kernel(w_ref, o_ref):
    lanes = V7X_LANES
    ab0 = A_QM_COL
    o_ref[:, 0:A_QM_COL] = w_ref[:, 0:A_QM_COL].astype(BF16)
    o_ref[:, A_QM_COL:A_AB_COL] = w_ref[:, ab0 + 2 * DN_HEADS:ab0 + 2 * DN_HEADS + MEM_W].astype(BF16)
    first = w_ref[:, ab0:ab0 + lanes]
    lane = lax.broadcasted_iota(jnp.int32, first.shape, 1)
    a = jnp.where(lane < DN_HEADS, first, 0.0)
    b = jnp.where(lane < DN_HEADS, pltpu.roll(first, lanes - DN_HEADS, axis=1), 0.0)
    o_ref[:, A_AB_COL:A_AB_COL + lanes] = a.astype(BF16)
    o_ref[:, A_AB_COL + lanes:] = b.astype(BF16)


def _a_proj_weights(w_in_all, layer):
    _, k, n = w_in_all.shape
    assert n == A_QM_COL + 2 * DN_HEADS + MEM_W
    tk = 256
    return pl.pallas_call(
        _a_weights_kernel,
        grid=(k // tk,),
        in_specs=[pl.BlockSpec((None, tk, n), lambda i: (layer, i, 0))],
        out_specs=pl.BlockSpec((tk, A_W_COLS), lambda i: (i, 0)),
        out_shape=jax.ShapeDtypeStruct((k, A_W_COLS), BF16),
        compiler_params=_cparams(1),
        name="a_weights",
    )(w_in_all)


def _tile(n, pref):
    t = min(n, pref)
    assert n % t == 0
    return t


def kernel(x, mem, positions, a_w_in, a_conv_w, a_A_log, a_dt_bias, a_norm_w, b_w_in, b_sinks,
           w_kv_shared, mem_w_kv, w_o, mlp_w_up, mlp_w_down, ln_g, ln_b):
    b, s, d = x.shape
    t = b * s
    m_tok = mem.shape[1]
    tm = _tile(s, ROW_TILE)
    mem2 = mem.reshape(b * m_tok, d)
    h = x
    cos_t = sin_t = kvar = vt = None
    w_o_b, b_w_in_b = w_o.astype(BF16), b_w_in.astype(BF16)
    w_up_b, w_down_b = mlp_w_up.astype(BF16), mlp_w_down.astype(BF16)
    for layer in range(DEPTH):
        mk, mv = _proj(mem2, mem_w_kv[layer].astype(BF16), ((0, MEM_W), (MEM_W, MEM_W)), (BF16, BF16),
                       b * m_tok)
        mk = mk.reshape(b, m_tok, MEM_W)
        mvt = jnp.swapaxes(mv.reshape(b, m_tok, MEM_W), 1, 2)
        if layer < N_A:
            qkv, z, qm, beta, gc, gr = _proj_a(h, _a_proj_weights(a_w_in, layer), a_conv_w[layer],
                                               a_A_log[layer], a_dt_bias[layer], tm)
            ts = _tile(s, GDN_TILE)
            o = _gdn(qkv, z, beta, gc, gr, a_norm_w[layer], ts)
        else:
            j = layer - N_A
            q_t, qm = _proj_b(h, b_w_in_b, j, cos_t, sin_t, tm)
            o = _swa(q_t, kvar, vt, b_sinks[j], _tile(s, SWA_TILE))
        h = _mix_out(o, qm, mk, mvt, w_o_b, layer, h, ln_g[layer, 0], ln_b[layer, 0], tm)
        if layer == N_A - 1:
            cos_t, sin_t = _rope_tables(positions)
            h, kvar, vt = _mlp(h.reshape(t, d), w_up_b, w_down_b, layer, ln_g[layer, 1], ln_b[layer, 1], tm,
                               shared_kv=(w_kv_shared.astype(BF16), cos_t, sin_t))
        else:
            (h,) = _mlp(h.reshape(t, d), w_up_b, w_down_b, layer, ln_g[layer, 1], ln_b[layer, 1], tm)
        h = h.reshape(b, s, d)
    return h
```

```python
import functools

import jax
import jax.numpy as jnp
from jax import lax
from jax.experimental import pallas as pl
from jax.experimental.pallas import tpu as pltpu

F32 = jnp.float32
BF16 = jnp.bfloat16

V7X_LANES = 128
V7X_VMEM_LIMIT_BYTES = 56 * 1024 * 1024
BF16_SUBLANES = 16

DEPTH = 4
N_A = DEPTH // 2
MEM_HEADS = 4
MEM_DH = 64
MEM_W = MEM_HEADS * MEM_DH
DN_DK = 128
DN_HEADS = 6
DN_PAIRS = DN_HEADS // 2
DN_W = DN_HEADS * DN_DK
CONV_WIDTH = 4
CHUNK = 64
SWA_DH = 64
SWA_HEADS = 12
SWA_KV_HEADS = 2
SWA_Q_W = SWA_HEADS * SWA_DH
SWA_PAIRS = SWA_Q_W // V7X_LANES
SWA_PAIRS_PER_KV = SWA_PAIRS // SWA_KV_HEADS
WINDOW = 128
ROPE_THETA = 10000.0
LN_EPS = 1e-5
NORM_EPS = 1e-6
DN_ALPHA = (2.0 * DEPTH) ** 0.25
LOG2_E = 1.4426950408889634

ROW_TILE = 1024
GDN_TILE = 1024
SWA_TILE = 1024


def _cparams(n_grid_axes):
    return pltpu.CompilerParams(
        dimension_semantics=("arbitrary",) * n_grid_axes,
        vmem_limit_bytes=V7X_VMEM_LIMIT_BYTES)


def _mm(a, b):
    return jnp.dot(a.astype(BF16), b.astype(BF16), preferred_element_type=F32)


def _mm_nt(a, b):
    return lax.dot_general(a.astype(BF16), b.astype(BF16), (((1,), (1,)), ((), ())),
                           preferred_element_type=F32)


def _sigmoid(x):
    return 1.0 / (1.0 + jnp.exp(-x))


def _layer_norm(y, g, b):
    mu = jnp.mean(y, axis=-1, keepdims=True)
    d = y - mu
    var = jnp.mean(d * d, axis=-1, keepdims=True)
    return d * lax.rsqrt(var + LN_EPS) * g + b


def _proj_kernel(x_ref, w_ref, *out_refs, segs, col_chunk):
    xb = x_ref[...].astype(BF16)
    for o_ref, (start, width) in zip(out_refs, segs):
        for c0 in range(0, width, col_chunk):
            c1 = min(c0 + col_chunk, width)
            acc = jnp.dot(xb, w_ref[:, start + c0:start + c1], preferred_element_type=F32)
            o_ref[:, c0:c1] = acc.astype(o_ref.dtype)


def _proj(x, w, segs, dtypes, tm):
    m, k = x.shape
    n = w.shape[1]
    assert m % tm == 0
    return pl.pallas_call(
        functools.partial(_proj_kernel, segs=segs, col_chunk=512),
        grid=(m // tm,),
        in_specs=[pl.BlockSpec((tm, k), lambda i: (i, 0)),
                  pl.BlockSpec((k, n), lambda i: (0, 0))],
        out_specs=[pl.BlockSpec((tm, wd), lambda i: (i, 0)) for _, wd in segs],
        out_shape=[jax.ShapeDtypeStruct((m, wd), dt) for (_, wd), dt in zip(segs, dtypes)],
        compiler_params=_cparams(1),
        name="proj",
    )(x, w)


ROPE_HALF = SWA_DH // 2
ROPE_PACK = V7X_LANES // ROPE_HALF


def _expand_rope(dense, dst_ref, signed):
    n = dense.shape[0]
    lane = lax.broadcasted_iota(jnp.int32, dense.shape, 1)
    group = lane // ROPE_HALF
    sign = jnp.where((group & 1) == 0, -1.0, 1.0)
    for k in range(ROPE_PACK):
        x = jnp.where(group == k, dense, 0.0)
        x = x + pltpu.roll(x, ROPE_HALF, axis=1)
        x = x + pltpu.roll(x, 2 * ROPE_HALF, axis=1)
        if signed:
            x = x * sign
        dst_ref[pl.ds(k, n, stride=ROPE_PACK), :] = x


def _rope_table_kernel(pos_ref, inv_ref, cos_ref, sin_ref):
    ang = pos_ref[...].astype(F32) * inv_ref[...]
    _expand_rope(jnp.cos(ang), cos_ref, False)
    _expand_rope(jnp.sin(ang), sin_ref, True)


def _rope_tables(positions):
    b, s = positions.shape
    inv_freq = ROPE_THETA ** (-jnp.arange(0, SWA_DH, 2, dtype=F32) / SWA_DH)
    pos_rep = jnp.broadcast_to(positions[..., None], (b, s, ROPE_HALF)).reshape(b * s // ROPE_PACK, V7X_LANES)
    inv_rep = jnp.tile(inv_freq, ROPE_PACK)[None, :]
    rows = pos_rep.shape[0]
    tr = min(rows, 1024)
    assert rows % tr == 0
    cos_t, sin_t = pl.pallas_call(
        _rope_table_kernel,
        grid=(rows // tr,),
        in_specs=[pl.BlockSpec((tr, V7X_LANES), lambda i: (i, 0)),
                  pl.BlockSpec((1, V7X_LANES), lambda i: (0, 0))],
        out_specs=[pl.BlockSpec((tr * ROPE_PACK, V7X_LANES), lambda i: (i, 0))] * 2,
        out_shape=[jax.ShapeDtypeStruct((rows * ROPE_PACK, V7X_LANES), F32)] * 2,
        compiler_params=_cparams(1),
        name="rope_tables",
    )(pos_rep, inv_rep)
    return cos_t.reshape(b, s, V7X_LANES), sin_t.reshape(b, s, V7X_LANES)


def _rope_pairs(x, cos_t, sin_t):
    lane = lax.broadcasted_iota(jnp.int32, x.shape, 1)
    first_half = (lane & (SWA_DH - 1)) < (SWA_DH // 2)
    swapped = jnp.where(first_half,
                        pltpu.roll(x, V7X_LANES - SWA_DH // 2, axis=1),
                        pltpu.roll(x, SWA_DH // 2, axis=1))
    return x * cos_t + swapped * sin_t


A_QM_COL = 4 * DN_W
A_AB_COL = A_QM_COL + MEM_W
A_W_COLS = A_AB_COL + 2 * V7X_LANES
CONV_PAD = 8
CONV_COLS = 2 * V7X_LANES


def _proj_a_kernel(h_ref, w_ref, cw_ref, alog_ref, dtb_ref,
                   qkv_ref, z_ref, qm_ref, beta_ref, gc_ref, gr_ref, ext_ref, carry_ref):
    tm = h_ref.shape[1]
    lanes = V7X_LANES

    @pl.when(pl.program_id(1) == 0)
    def _():
        carry_ref[...] = jnp.zeros_like(carry_ref)

    xb = h_ref[0].astype(BF16)

    def matmul(c0, width):
        return jnp.dot(xb, w_ref[:, c0:c0 + width], preferred_element_type=F32)

    def conv_block(c0):
        for half in range(CONV_COLS // lanes):
            cc = c0 + half * lanes
            y = cw_ref[CONV_WIDTH - 1:CONV_WIDTH, cc:cc + lanes] * ext_ref[CONV_PAD:CONV_PAD + tm, cc:cc + lanes]
            for j in range(CONV_WIDTH - 1):
                r0 = CONV_PAD - (CONV_WIDTH - 1) + j
                y = y + cw_ref[j:j + 1, cc:cc + lanes] * ext_ref[r0:r0 + tm, cc:cc + lanes]
            y = y * _sigmoid(y)
            if cc < 2 * DN_W:
                y = y * lax.rsqrt(jnp.sum(y * y, axis=-1, keepdims=True) + NORM_EPS)
            qkv_ref[0, :, cc:cc + lanes] = y.astype(qkv_ref.dtype)

    starts = list(range(0, 3 * DN_W, CONV_COLS))
    ext_ref[0:CONV_PAD, :] = carry_ref[...]
    for c0 in starts:
        acc = matmul(c0, CONV_COLS)
        ext_ref[CONV_PAD:CONV_PAD + tm, c0:c0 + CONV_COLS] = acc
        carry_ref[:, c0:c0 + CONV_COLS] = acc[tm - CONV_PAD:tm, :]

    for c0 in range(0, DN_W, CONV_COLS):
        z_ref[0, :, c0:c0 + CONV_COLS] = matmul(3 * DN_W + c0, CONV_COLS).astype(z_ref.dtype)
    qm_ref[0] = matmul(A_QM_COL, MEM_W).astype(qm_ref.dtype)

    ab = matmul(A_AB_COL, 2 * lanes)
    for c0 in starts:
        conv_block(c0)
    xg = ab[:, :lanes] + dtb_ref[...]
    softplus = jnp.maximum(xg, 0.0) + jnp.log1p(jnp.exp(-jnp.abs(xg)))
    g = -jnp.exp(alog_ref[...]) * softplus
    beta_ref[0] = _sigmoid(ab[:, lanes:])

    row = lax.broadcasted_iota(jnp.int32, g.shape, 0) & (CHUNK - 1)
    gc = g
    step = 1
    while step < CHUNK:
        gc = gc + jnp.where(row >= step, pltpu.roll(gc, step, axis=0), 0.0)
        step *= 2
    gc_ref[0] = gc

    gct = jnp.transpose(gc)[0:8, :]
    plus = pltpu.roll(pltpu.roll(gct, CHUNK, axis=1), 7, axis=0)
    minus = pltpu.roll(pltpu.roll(gct, tm - CHUNK, axis=1), 1, axis=0)
    lane = lax.broadcasted_iota(jnp.int32, gct.shape, 1)
    low = (lane & (lanes - 1)) < CHUNK
    even = (lax.broadcasted_iota(jnp.int32, gct.shape, 0) & 1) == 0
    gr_ref[0] = jnp.where(even, jnp.where(low, gct, plus), jnp.where(low, minus, gct))


def _proj_a(h, w, conv_w, a_log, dt_bias, tm):
    b, s, d = h.shape
    assert s % tm == 0 and tm % V7X_LANES == 0 and w.shape == (d, A_W_COLS)
    pad = V7X_LANES - DN_HEADS
    alog = jnp.pad(a_log.astype(F32), (0, pad))[None, :]
    dtb = jnp.pad(dt_bias.astype(F32), (0, pad))[None, :]
    cur = lambda bi, i: (bi, i, 0)
    const = lambda bi, i: (0, 0)
    lanes = V7X_LANES
    return pl.pallas_call(
        _proj_a_kernel,
        grid=(b, s // tm),
        in_specs=[
            pl.BlockSpec((1, tm, d), cur),
            pl.BlockSpec((d, A_W_COLS), const),
            pl.BlockSpec((CONV_WIDTH, 3 * DN_W), const),
            pl.BlockSpec((1, lanes), const),
            pl.BlockSpec((1, lanes), const),
        ],
        out_specs=[
            pl.BlockSpec((1, tm, 3 * DN_W), cur),
            pl.BlockSpec((1, tm, DN_W), cur),
            pl.BlockSpec((1, tm, MEM_W), cur),
            pl.BlockSpec((1, tm, lanes), cur),
            pl.BlockSpec((1, tm, lanes), cur),
            pl.BlockSpec((1, 8, tm), lambda bi, i: (bi, 0, i)),
        ],
        out_shape=[
            jax.ShapeDtypeStruct((b, s, 3 * DN_W), BF16),
            jax.ShapeDtypeStruct((b, s, DN_W), BF16),
            jax.ShapeDtypeStruct((b, s, MEM_W), BF16),
            jax.ShapeDtypeStruct((b, s, lanes), F32),
            jax.ShapeDtypeStruct((b, s, lanes), F32),
            jax.ShapeDtypeStruct((b, 8, s), F32),
        ],
        scratch_shapes=[pltpu.VMEM((CONV_PAD + tm, 3 * DN_W), F32),
                        pltpu.VMEM((CONV_PAD, 3 * DN_W), F32)],
        compiler_params=_cparams(2),
        name="proj_a",
    )(h, w, conv_w.astype(F32), alog, dtb)


def _gdn_kernel(qkv_ref, z_ref, beta_ref, gc_ref, gr_ref, nw_ref, o_ref, state_ref):
    ts = qkv_ref.shape[1]
    c = CHUNK
    lanes = V7X_LANES

    @pl.when(pl.program_id(1) == 0)
    def _():
        state_ref[...] = jnp.zeros_like(state_ref)

    lane2 = lax.broadcasted_iota(jnp.int32, (c, 2 * lanes), 1)
    head_a2 = lane2 < lanes
    lane1 = lax.broadcasted_iota(jnp.int32, (c, lanes), 1)
    half_a = lane1 < c
    col = lane1 & (c - 1)
    row = lax.broadcasted_iota(jnp.int32, (c, lanes), 0)
    incl = row >= col
    strict = row > col
    eye = jnp.where(row == col, 1.0, 0.0)
    lane_sq = lax.broadcasted_iota(jnp.int32, (lanes, lanes), 1)
    half_a_sq = lane_sq < c
    nw = nw_ref[...]
    q_scale = DN_DK ** -0.5

    def split_halves(x, mask):
        return jnp.concatenate([jnp.where(mask, x, 0.0), jnp.where(mask, 0.0, x)], axis=0)

    n_sq = c.bit_length() - 1
    states = [state_ref[hd] for hd in range(DN_HEADS)]

    def process(items):
        xs, invs, intras, rhss, kdts, qes, cds = [], [], [], [], [], [], []
        for p, ci in items:
            ha, hb = 2 * p, 2 * p + 1
            r0 = ci * c
            qp = qkv_ref[0, r0:r0 + c, 2 * lanes * p:2 * lanes * (p + 1)].astype(F32)
            kp = qkv_ref[0, r0:r0 + c, DN_W + 2 * lanes * p:DN_W + 2 * lanes * (p + 1)].astype(F32)
            vp = qkv_ref[0, r0:r0 + c, 2 * DN_W + 2 * lanes * p:2 * DN_W + 2 * lanes * (p + 1)].astype(F32)
            beta_t = beta_ref[0, r0:r0 + c, :]
            gc_t = gc_ref[0, r0:r0 + c, :]
            b_a, b_b = beta_t[:, ha:ha + 1], beta_t[:, hb:hb + 1]
            g_a, g_b = gc_t[:, ha:ha + 1], gc_t[:, hb:hb + 1]
            beta2 = jnp.where(head_a2, b_a, b_b)
            g2 = jnp.where(head_a2, g_a, g_b)
            g1 = jnp.where(half_a, g_a, g_b)
            g_row = gr_ref[0, 2 * p + (ci % 2):2 * p + (ci % 2) + 1, lanes * (ci // 2):lanes * (ci // 2 + 1)]
            g_last = g2[c - 1:c, :]
            eg = jnp.exp(g2)
            kb = kp * beta2
            vb = vp * beta2
            kbe = kb * eg
            qs = qp * q_scale
            kd = kp * jnp.exp(g_last - g2)
            qes.append(qs * eg)
            cds.append(jnp.exp(g_last))

            r1 = _mm_nt(jnp.concatenate([kb, qs], axis=0), split_halves(kp, head_a2))
            decay = jnp.where(incl, jnp.exp(g1 - g_row), 0.0)
            x = jnp.where(strict, -(r1[:c] * decay), 0.0)
            xs.append(x)
            invs.append(eye + x)
            intras.append(r1[c:] * decay)
            rhss.append(jnp.concatenate([
                jnp.concatenate([vb[:, :lanes], kbe[:, :lanes]], axis=1),
                jnp.concatenate([vb[:, lanes:], kbe[:, lanes:]], axis=1)], axis=0).astype(BF16))
            kdts.append(jnp.transpose(jnp.concatenate([kd[:, :lanes], kd[:, lanes:]], axis=0)))

        xs = [_mm(x, split_halves(x, half_a)) for x in xs]
        for _ in range(n_sq - 2):
            rs = [_mm(jnp.concatenate([x, inv], axis=0), split_halves(x, half_a)) for x, inv in zip(xs, invs)]
            xs = [r[:c] for r in rs]
            invs = [inv + r[c:] for inv, r in zip(invs, rs)]
        invs = [inv + _mm(inv, split_halves(x, half_a)) for x, inv in zip(xs, invs)]

        sols = [_mm(split_halves(inv, half_a), rhs) for inv, rhs in zip(invs, rhss)]
        r3s = [_mm(jnp.concatenate([split_halves(intra, half_a), split_halves(kdt, half_a_sq)], axis=0), sol)
               for intra, kdt, sol in zip(intras, kdts, sols)]

        for (p, ci), r3, qe, cd in zip(items, r3s, qes, cds):
            r0 = ci * c
            for hx in range(2):
                hd = 2 * p + hx
                lo = hx * lanes
                s_x = states[hd]
                o0 = r3[hx * c:(hx + 1) * c, :lanes]
                iw = r3[hx * c:(hx + 1) * c, lanes:]
                n0 = r3[2 * c + hx * lanes:2 * c + (hx + 1) * lanes, :lanes]
                kw = r3[2 * c + hx * lanes:2 * c + (hx + 1) * lanes, lanes:]
                r4 = _mm(jnp.concatenate([qe[:, lo:lo + lanes] - iw, -kw], axis=0), s_x)
                o = r4[:c] + o0
                states[hd] = cd[:, lo:lo + lanes] * s_x + n0 + r4[c:]
                zt = z_ref[0, r0:r0 + c, hd * lanes:(hd + 1) * lanes].astype(F32)
                on = o * lax.rsqrt(jnp.mean(o * o, axis=-1, keepdims=True) + NORM_EPS) * nw
                o_ref[0, r0:r0 + c, hd * lanes:(hd + 1) * lanes] = (on * (zt * _sigmoid(zt))).astype(o_ref.dtype)

    process([(p, ci) for ci in range(ts // c) for p in range(DN_PAIRS)])
    for hd in range(DN_HEADS):
        state_ref[hd] = states[hd]


def _gdn(qkv, z, beta, gc, gr, norm_w, ts):
    b, s, w = qkv.shape
    return pl.pallas_call(
        _gdn_kernel,
        grid=(b, s // ts),
        in_specs=[
            pl.BlockSpec((1, ts, w), lambda bi, i: (bi, i, 0)),
            pl.BlockSpec((1, ts, DN_W), lambda bi, i: (bi, i, 0)),
            pl.BlockSpec((1, ts, V7X_LANES), lambda bi, i: (bi, i, 0)),
            pl.BlockSpec((1, ts, V7X_LANES), lambda bi, i: (bi, i, 0)),
            pl.BlockSpec((1, 8, ts), lambda bi, i: (bi, 0, i)),
            pl.BlockSpec((1, V7X_LANES), lambda bi, i: (0, 0)),
        ],
        out_specs=pl.BlockSpec((1, ts, DN_W), lambda bi, i: (bi, i, 0)),
        out_shape=jax.ShapeDtypeStruct((b, s, DN_W), BF16),
        scratch_shapes=[pltpu.VMEM((DN_HEADS, DN_DK, DN_DK), F32)],
        compiler_params=_cparams(2),
        name="gdn",
    )(qkv, z, beta, gc, gr, norm_w.astype(F32)[None, :])


SWA_Q_SCALE = SWA_DH ** -0.5 * LOG2_E


def _proj_b_kernel(h_ref, w_ref, cos_ref, sin_ref, qt_ref, qm_ref):
    lanes = V7X_LANES
    xb = h_ref[0].astype(BF16)
    cos_t = cos_ref[0]
    sin_t = sin_ref[0]
    for c0 in range(0, SWA_Q_W, 2 * lanes):
        acc = jnp.dot(xb, w_ref[:, c0:c0 + 2 * lanes], preferred_element_type=F32)
        for half in range(2):
            q = _rope_pairs(acc[:, half * lanes:(half + 1) * lanes], cos_t, sin_t) * SWA_Q_SCALE
            qt_ref[0, c0 // lanes + half] = jnp.transpose(q).astype(BF16)
    qm_ref[0] = jnp.dot(xb, w_ref[:, SWA_Q_W:], preferred_element_type=F32).astype(qm_ref.dtype)


def _proj_b(h, w, layer, cos_t, sin_t, tm):
    b, s, d = h.shape
    lanes = V7X_LANES
    cur = lambda bi, i: (bi, i, 0)
    return pl.pallas_call(
        _proj_b_kernel,
        grid=(b, s // tm),
        in_specs=[
            pl.BlockSpec((1, tm, d), cur),
            pl.BlockSpec((None,) + w.shape[1:], lambda bi, i: (layer, 0, 0)),
            pl.BlockSpec((1, tm, lanes), cur),
            pl.BlockSpec((1, tm, lanes), cur),
        ],
        out_specs=[
            pl.BlockSpec((1, SWA_PAIRS, lanes, tm), lambda bi, i: (bi, 0, 0, i)),
            pl.BlockSpec((1, tm, MEM_W), cur),
        ],
        out_shape=[
            jax.ShapeDtypeStruct((b, SWA_PAIRS, lanes, s), BF16),
            jax.ShapeDtypeStruct((b, s, MEM_W), BF16),
        ],
        compiler_params=_cparams(2),
        name="proj_b",
    )(h, w, cos_t, sin_t)


def _emit_shared_kv(h_tile, w_ref, cos_t, sin_t, kvar_ref, vt_ref):
    lanes = V7X_LANES
    acc = jnp.dot(h_tile.astype(BF16), w_ref[...], preferred_element_type=F32)
    k_rot = _rope_pairs(acc[:, :lanes], cos_t, sin_t)
    v = acc[:, lanes:]
    lane = lax.broadcasted_iota(jnp.int32, k_rot.shape, 1)
    low = lane < SWA_DH
    k_sw = pltpu.roll(k_rot, SWA_DH, axis=1)
    kvar_ref[0, 0] = jnp.where(low, k_rot, 0.0).astype(BF16)
    kvar_ref[0, 1] = jnp.where(low, 0.0, k_sw).astype(BF16)
    kvar_ref[0, 2] = jnp.where(low, k_sw, 0.0).astype(BF16)
    kvar_ref[0, 3] = jnp.where(low, 0.0, k_rot).astype(BF16)
    v_sw = pltpu.roll(v, SWA_DH, axis=1)
    ones_hi = jnp.where(lane == SWA_DH, 1.0, 0.0)
    ones_lo = jnp.where(lane == 0, 1.0, 0.0)
    vt_ref[0, 0] = jnp.transpose(jnp.where(low, v, ones_hi)).astype(BF16)
    vt_ref[0, 1] = jnp.transpose(jnp.where(low, ones_lo, v_sw)).astype(BF16)
    vt_ref[0, 2] = jnp.transpose(jnp.where(low, v_sw, ones_hi)).astype(BF16)
    vt_ref[0, 3] = jnp.transpose(jnp.where(low, ones_lo, v)).astype(BF16)


def _swa_kernel(sink_ref, q_ref, kv_ref, kvh_ref, vt_ref, vth_ref, o_ref):
    tq = q_ref.shape[3]
    lanes = V7X_LANES
    w = WINDOW
    first_key = jnp.where(pl.program_id(1) == 0, w, 0)
    sum_row = (SWA_DH, 0)

    def key_rows(var, j):
        if j == 0:
            return jnp.concatenate([kvh_ref[0, var], kv_ref[0, var, 0:w, :]], axis=0)
        return kv_ref[0, var, (j - 1) * w:(j + 1) * w, :]

    def value_cols(var, j):
        if j == 0:
            return jnp.concatenate([vth_ref[0, var], vt_ref[0, var, :, 0:w]], axis=1)
        return vt_ref[0, var, :, (j - 1) * w:(j + 1) * w]

    cols3 = SWA_PAIRS_PER_KV * w
    kj = lax.broadcasted_iota(jnp.int32, (2 * w, cols3), 0)
    qi = lax.broadcasted_iota(jnp.int32, (2 * w, cols3), 1) & (w - 1)
    dist = qi + w - kj
    band = (dist >= 0) & (dist < w)
    neg_inf = jnp.float32(-jnp.inf)
    bias_band = jnp.where(band, 0.0, neg_inf)
    bias_first = jnp.where(band & (kj >= first_key), 0.0, neg_inf)
    head_slot = lax.broadcasted_iota(jnp.int32, (1, cols3), 1) // w
    row_low = lax.broadcasted_iota(jnp.int32, (lanes, cols3), 0) < SWA_DH
    combos = [(g, e) for g in range(SWA_KV_HEADS) for e in range(2)]
    group_pairs = [[SWA_PAIRS_PER_KV * g + t for t in range(SWA_PAIRS_PER_KV)] for g in range(SWA_KV_HEADS)]

    def scores(j):
        r0 = j * w
        out = []
        for g, e in combos:
            qs = jnp.concatenate([q_ref[0, p, :, r0:r0 + w] for p in group_pairs[g]], axis=1)
            out.append(jnp.dot(key_rows(2 * g + e, j), qs, preferred_element_type=F32))
        return out

    def finish(j, ss):
        r0 = j * w
        bias = bias_first if j == 0 else bias_band
        pes, sink_terms = [], []
        for (g, e), s in zip(combos, ss):
            s = s + bias
            hs = [2 * p + e for p in group_pairs[g]]
            sink = jnp.where(head_slot == 0, sink_ref[hs[0]],
                             jnp.where(head_slot == 1, sink_ref[hs[1]], sink_ref[hs[2]])) * LOG2_E
            m = jnp.maximum(jnp.max(s, axis=0, keepdims=True), sink)
            pes.append(jnp.exp2(s - m).astype(BF16))
            sink_terms.append(jnp.exp2(sink - m))
        pvs = [jnp.dot(value_cols(2 * g + e, j), pe, preferred_element_type=F32)
               for (g, e), pe in zip(combos, pes)]
        for g in range(SWA_KV_HEADS):
            normed = []
            for e in range(2):
                pv = pvs[2 * g + e]
                den = pv[sum_row[e]:sum_row[e] + 1, :] + sink_terms[2 * g + e]
                normed.append(pv * (1.0 / den))
            acc = jnp.where(row_low, normed[0], normed[1])
            for t, p in enumerate(group_pairs[g]):
                o_ref[0, r0:r0 + w, p * lanes:(p + 1) * lanes] = (
                    jnp.transpose(acc[:, t * w:(t + 1) * w]).astype(o_ref.dtype))

    n_blocks = tq // w
    ss = scores(0)
    for j in range(n_blocks):
        nxt = scores(j + 1) if j + 1 < n_blocks else None
        finish(j, ss)
        ss = nxt


def _swa(q_t, kvar, vt, sinks, tq):
    b, _, lanes, s = q_t.shape
    assert s % tq == 0 and tq % WINDOW == 0
    hb = tq // WINDOW
    prev = lambda i: jnp.maximum(i * hb - 1, 0)
    return pl.pallas_call(
        _swa_kernel,
        grid=(b, s // tq),
        in_specs=[
            pl.BlockSpec(memory_space=pltpu.SMEM),
            pl.BlockSpec((1, SWA_PAIRS, lanes, tq), lambda bi, i: (bi, 0, 0, i)),
            pl.BlockSpec((1, 4, tq, lanes), lambda bi, i: (bi, 0, i, 0)),
            pl.BlockSpec((1, 4, WINDOW, lanes), lambda bi, i: (bi, 0, prev(i), 0)),
            pl.BlockSpec((1, 4, lanes, tq), lambda bi, i: (bi, 0, 0, i)),
            pl.BlockSpec((1, 4, lanes, WINDOW), lambda bi, i: (bi, 0, 0, prev(i))),
        ],
        out_specs=pl.BlockSpec((1, tq, SWA_Q_W), lambda bi, i: (bi, i, 0)),
        out_shape=jax.ShapeDtypeStruct((b, s, SWA_Q_W), BF16),
        compiler_params=_cparams(2),
        name="swa",
    )(sinks.astype(F32), q_t, kvar, kvar, vt, vt)


def _mix_out_kernel(o_ref, qm_ref, mk_ref, mvt_ref, wo_ref, h_ref, g_ref, b_ref, out_ref):
    mk = mk_ref[0].astype(F32)
    mvt = mvt_ref[0]
    lane_head = lax.broadcasted_iota(jnp.int32, mk.shape, 1) // MEM_DH
    heads = range(MEM_HEADS)
    khs = [jnp.where(lane_head == hd, mk, 0.0).astype(BF16) for hd in heads]
    ones = jnp.ones((BF16_SUBLANES, mvt.shape[1]), BF16)
    vhs = [jnp.concatenate([mvt[hd * MEM_DH:(hd + 1) * MEM_DH, :], ones], axis=0) for hd in heads]
    dn = o_ref.shape[2]
    qm = qm_ref[0].astype(F32) * (MEM_DH ** -0.5 * LOG2_E)
    qm_t = jnp.transpose(qm).astype(BF16)
    ss = [jnp.dot(kh, qm_t, preferred_element_type=F32) for kh in khs]
    half = qm_t.shape[1] // 2
    mix_a = jnp.dot(o_ref[0, 0:half, :], wo_ref[0:dn, :], preferred_element_type=F32)
    pes = [jnp.exp2(s - jnp.max(s, axis=0, keepdims=True)).astype(BF16) for s in ss]
    parts = []
    for vh, pe in zip(vhs, pes):
        pv = jnp.dot(vh, pe, preferred_element_type=F32)
        parts.append(pv[:MEM_DH] * (1.0 / pv[MEM_DH:MEM_DH + 1]))
    mix_b = jnp.dot(o_ref[0, half:, :], wo_ref[0:dn, :], preferred_element_type=F32)
    mix = jnp.concatenate([mix_a, mix_b], axis=0)
    mo = jnp.transpose(jnp.concatenate(parts, axis=0))
    mix = mix + jnp.dot(mo.astype(BF16), wo_ref[dn:, :], preferred_element_type=F32)
    out_ref[0] = _layer_norm(DN_ALPHA * h_ref[0] + mix, g_ref[...], b_ref[...])


def _mix_out(o, qm, mk, mvt, wo, layer, h, ln_g, ln_b, tm):
    b, s, d = h.shape
    m_tok = mk.shape[1]
    cur = lambda bi, i: (bi, i, 0)
    per_b = lambda bi, i: (bi, 0, 0)
    const = lambda bi, i: (0, 0)
    return pl.pallas_call(
        _mix_out_kernel,
        grid=(b, s // tm),
        in_specs=[
            pl.BlockSpec((1, tm, o.shape[2]), cur),
            pl.BlockSpec((1, tm, MEM_W), cur),
            pl.BlockSpec((1, m_tok, MEM_W), per_b),
            pl.BlockSpec((1, MEM_W, m_tok), per_b),
            pl.BlockSpec((None,) + wo.shape[1:], lambda bi, i: (layer, 0, 0)),
            pl.BlockSpec((1, tm, d), cur),
            pl.BlockSpec((1, d), const),
            pl.BlockSpec((1, d), const),
        ],
        out_specs=pl.BlockSpec((1, tm, d), cur),
        out_shape=jax.ShapeDtypeStruct((b, s, d), F32),
        compiler_params=_cparams(2),
        name="mix_out",
    )(o, qm, mk, mvt, wo, h, ln_g[None, :], ln_b[None, :])


def _mlp_kernel(x_ref, up_ref, dn_ref, g_ref, b_ref, *rest, hid_chunk, with_kv):
    o_ref = rest[3] if with_kv else rest[0]
    x = x_ref[...]
    xb = x.astype(BF16)
    hidden = up_ref.shape[1]
    acc = DN_ALPHA * x
    for j in range(0, hidden, hid_chunk):
        hid = jnp.dot(xb, up_ref[:, j:j + hid_chunk], preferred_element_type=F32)
        hid = jnp.square(jnp.maximum(hid, 0.0))
        acc = acc + jnp.dot(hid.astype(BF16), dn_ref[j:j + hid_chunk, :], preferred_element_type=F32)
    out = _layer_norm(acc, g_ref[...], b_ref[...])
    o_ref[...] = out
    if with_kv:
        wkv_ref, cos_ref, sin_ref, _, kvar_ref, vt_ref = rest
        _emit_shared_kv(out, wkv_ref, cos_ref[0], sin_ref[0], kvar_ref, vt_ref)


def _mlp(x, w_up, w_down, layer, ln_g, ln_b, tm, shared_kv=None):
    m, d = x.shape
    hidden = w_up.shape[2]
    lanes = V7X_LANES
    in_specs = [
        pl.BlockSpec((tm, d), lambda i: (i, 0)),
        pl.BlockSpec((None, d, hidden), lambda i: (layer, 0, 0), pipeline_mode=pl.Buffered(1)),
        pl.BlockSpec((None, hidden, d), lambda i: (layer, 0, 0), pipeline_mode=pl.Buffered(1)),
        pl.BlockSpec((1, d), lambda i: (0, 0)),
        pl.BlockSpec((1, d), lambda i: (0, 0)),
    ]
    out_specs = [pl.BlockSpec((tm, d), lambda i: (i, 0))]
    out_shape = [jax.ShapeDtypeStruct((m, d), F32)]
    args = [x, w_up, w_down, ln_g[None, :], ln_b[None, :]]
    if shared_kv is not None:
        w_kv, cos_t, sin_t = shared_kv
        b, s, _ = cos_t.shape
        nb = s // tm
        assert b * s == m and s % tm == 0
        in_specs += [pl.BlockSpec(w_kv.shape, lambda i: (0, 0)),
                     pl.BlockSpec((1, tm, lanes), lambda i: (i // nb, i % nb, 0)),
                     pl.BlockSpec((1, tm, lanes), lambda i: (i // nb, i % nb, 0))]
        out_specs += [pl.BlockSpec((1, 4, tm, lanes), lambda i: (i // nb, 0, i % nb, 0)),
                      pl.BlockSpec((1, 4, lanes, tm), lambda i: (i // nb, 0, 0, i % nb))]
        out_shape += [jax.ShapeDtypeStruct((b, 4, s, lanes), BF16),
                      jax.ShapeDtypeStruct((b, 4, lanes, s), BF16)]
        args += [w_kv, cos_t, sin_t]
    return pl.pallas_call(
        functools.partial(_mlp_kernel, hid_chunk=1024, with_kv=shared_kv is not None),
        grid=(m // tm,),
        in_specs=in_specs,
        out_specs=out_specs,
        out_shape=out_shape,
        compiler_params=_cparams(1),
        name="mlp",
    )(*args)


def _a_weights_kernel(w_ref, o_ref):
    lanes = V7X_LANES
    ab0 = A_QM_COL
    o_ref[:, 0:A_QM_COL] = w_ref[:, 0:A_QM_COL].astype(BF16)
    o_ref[:, A_QM_COL:A_AB_COL] = w_ref[:, ab0 + 2 * DN_HEADS:ab0 + 2 * DN_HEADS + MEM_W].astype(BF16)
    first = w_ref[:, ab0:ab0 + lanes]
    lane = lax.broadcasted_iota(jnp.int32, first.shape, 1)
    a = jnp.where(lane < DN_HEADS, first, 0.0)
    b = jnp.where(lane < DN_HEADS, pltpu.roll(first, lanes - DN_HEADS, axis=1), 0.0)
    o_ref[:, A_AB_COL:A_AB_COL + lanes] = a.astype(BF16)
    o_ref[:, A_AB_COL + lanes:] = b.astype(BF16)


def _a_proj_weights(w_in_all, layer):
    _, k, n = w_in_all.shape
    assert n == A_QM_COL + 2 * DN_HEADS + MEM_W
    tk = 256
    return pl.pallas_call(
        _a_weights_kernel,
        grid=(k // tk,),
        in_specs=[pl.BlockSpec((None, tk, n), lambda i: (layer, i, 0))],
        out_specs=pl.BlockSpec((tk, A_W_COLS), lambda i: (i, 0)),
        out_shape=jax.ShapeDtypeStruct((k, A_W_COLS), BF16),
        compiler_params=_cparams(1),
        name="a_weights",
    )(w_in_all)


def _tile(n, pref):
    t = min(n, pref)
    assert n % t == 0
    return t


def kernel(x, mem, positions, a_w_in, a_conv_w, a_A_log, a_dt_bias, a_norm_w, b_w_in, b_sinks,
           w_kv_shared, mem_w_kv, w_o, mlp_w_up, mlp_w_down, ln_g, ln_b):
    b, s, d = x.shape
    t = b * s
    m_tok = mem.shape[1]
    tm = _tile(s, ROW_TILE)
    mem2 = mem.reshape(b * m_tok, d)
    h = x
    cos_t = sin_t = kvar = vt = None
    w_o_b, b_w_in_b = w_o.astype(BF16), b_w_in.astype(BF16)
    w_up_b, w_down_b = mlp_w_up.astype(BF16), mlp_w_down.astype(BF16)
    for layer in range(DEPTH):
        mk, mv = _proj(mem2, mem_w_kv[layer].astype(BF16), ((0, MEM_W), (MEM_W, MEM_W)), (BF16, BF16),
                       b * m_tok)
        mk = mk.reshape(b, m_tok, MEM_W)
        mvt = jnp.swapaxes(mv.reshape(b, m_tok, MEM_W), 1, 2)
        if layer < N_A:
            qkv, z, qm, beta, gc, gr = _proj_a(h, _a_proj_weights(a_w_in, layer), a_conv_w[layer],
                                               a_A_log[layer], a_dt_bias[layer], tm)
            ts = _tile(s, GDN_TILE)
            o = _gdn(qkv, z, beta, gc, gr, a_norm_w[layer], ts)
        else:
            j = layer - N_A
            q_t, qm = _proj_b(h, b_w_in_b, j, cos_t, sin_t, tm)
            o = _swa(q_t, kvar, vt, b_sinks[j], _tile(s, SWA_TILE))
        h = _mix_out(o, qm, mk, mvt, w_o_b, layer, h, ln_g[layer, 0], ln_b[layer, 0], tm)
        if layer == N_A - 1:
            cos_t, sin_t = _rope_tables(positions)
            h, kvar, vt = _mlp(h.reshape(t, d), w_up_b, w_down_b, layer, ln_g[layer, 1], ln_b[layer, 1], tm,
                               shared_kv=(w_kv_shared.astype(BF16), cos_t, sin_t))
        else:
            (h,) = _mlp(h.reshape(t, d), w_up_b, w_down_b, layer, ln_g[layer, 1], ln_b[layer, 1], tm)
        h = h.reshape(b, s, d)
    return h
```

```python
import functools

import jax
import jax.numpy as jnp
from jax import lax
from jax.experimental import pallas as pl
from jax.experimental.pallas import tpu as pltpu

F32 = jnp.float32
BF16 = jnp.bfloat16

V7X_LANES = 128
V7X_VMEM_LIMIT_BYTES = 56 * 1024 * 1024
BF16_SUBLANES = 16

DEPTH = 4
N_A = DEPTH // 2
MEM_HEADS = 4
MEM_DH = 64
MEM_W = MEM_HEADS * MEM_DH
DN_DK = 128
DN_HEADS = 6
DN_PAIRS = DN_HEADS // 2
DN_W = DN_HEADS * DN_DK
CONV_WIDTH = 4
CHUNK = 64
SWA_DH = 64
SWA_HEADS = 12
SWA_KV_HEADS = 2
SWA_Q_W = SWA_HEADS * SWA_DH
SWA_PAIRS = SWA_Q_W // V7X_LANES
SWA_PAIRS_PER_KV = SWA_PAIRS // SWA_KV_HEADS
WINDOW = 128
ROPE_THETA = 10000.0
LN_EPS = 1e-5
NORM_EPS = 1e-6
DN_ALPHA = (2.0 * DEPTH) ** 0.25
LOG2_E = 1.4426950408889634

ROW_TILE = 1024
GDN_TILE = 1024
SWA_TILE = 2048


def _cparams(n_grid_axes):
    return pltpu.CompilerParams(
        dimension_semantics=("arbitrary",) * n_grid_axes,
        vmem_limit_bytes=V7X_VMEM_LIMIT_BYTES)


def _mm(a, b):
    return jnp.dot(a.astype(BF16), b.astype(BF16), preferred_element_type=F32)


def _mm_nt(a, b):
    return lax.dot_general(a.astype(BF16), b.astype(BF16), (((1,), (1,)), ((), ())),
                           preferred_element_type=F32)


def _sigmoid(x):
    return 1.0 / (1.0 + jnp.exp(-x))


def _layer_norm(y, g, b):
    mu = jnp.mean(y, axis=-1, keepdims=True)
    d = y - mu
    var = jnp.mean(d * d, axis=-1, keepdims=True)
    return d * lax.rsqrt(var + LN_EPS) * g + b


def _proj_kernel(x_ref, w_ref, *out_refs, segs, col_chunk):
    xb = x_ref[...].astype(BF16)
    for o_ref, (start, width) in zip(out_refs, segs):
        for c0 in range(0, width, col_chunk):
            c1 = min(c0 + col_chunk, width)
            acc = jnp.dot(xb, w_ref[:, start + c0:start + c1], preferred_element_type=F32)
            o_ref[:, c0:c1] = acc.astype(o_ref.dtype)


def _proj(x, w, segs, dtypes, tm):
    m, k = x.shape
    n = w.shape[1]
    assert m % tm == 0
    return pl.pallas_call(
        functools.partial(_proj_kernel, segs=segs, col_chunk=512),
        grid=(m // tm,),
        in_specs=[pl.BlockSpec((tm, k), lambda i: (i, 0)),
                  pl.BlockSpec((k, n), lambda i: (0, 0))],
        out_specs=[pl.BlockSpec((tm, wd), lambda i: (i, 0)) for _, wd in segs],
        out_shape=[jax.ShapeDtypeStruct((m, wd), dt) for (_, wd), dt in zip(segs, dtypes)],
        compiler_params=_cparams(1),
        name="proj",
    )(x, w)


ROPE_HALF = SWA_DH // 2
ROPE_PACK = V7X_LANES // ROPE_HALF


def _expand_rope(dense, dst_ref, signed):
    n = dense.shape[0]
    lane = lax.broadcasted_iota(jnp.int32, dense.shape, 1)
    group = lane // ROPE_HALF
    sign = jnp.where((group & 1) == 0, -1.0, 1.0)
    for k in range(ROPE_PACK):
        x = jnp.where(group == k, dense, 0.0)
        x = x + pltpu.roll(x, ROPE_HALF, axis=1)
        x = x + pltpu.roll(x, 2 * ROPE_HALF, axis=1)
        if signed:
            x = x * sign
        dst_ref[pl.ds(k, n, stride=ROPE_PACK), :] = x


def _rope_table_kernel(pos_ref, inv_ref, cos_ref, sin_ref):
    ang = pos_ref[...].astype(F32) * inv_ref[...]
    _expand_rope(jnp.cos(ang), cos_ref, False)
    _expand_rope(jnp.sin(ang), sin_ref, True)


def _rope_tables(positions):
    b, s = positions.shape
    inv_freq = ROPE_THETA ** (-jnp.arange(0, SWA_DH, 2, dtype=F32) / SWA_DH)
    pos_rep = jnp.broadcast_to(positions[..., None], (b, s, ROPE_HALF)).reshape(b * s // ROPE_PACK, V7X_LANES)
    inv_rep = jnp.tile(inv_freq, ROPE_PACK)[None, :]
    rows = pos_rep.shape[0]
    tr = min(rows, 1024)
    assert rows % tr == 0
    cos_t, sin_t = pl.pallas_call(
        _rope_table_kernel,
        grid=(rows // tr,),
        in_specs=[pl.BlockSpec((tr, V7X_LANES), lambda i: (i, 0)),
                  pl.BlockSpec((1, V7X_LANES), lambda i: (0, 0))],
        out_specs=[pl.BlockSpec((tr * ROPE_PACK, V7X_LANES), lambda i: (i, 0))] * 2,
        out_shape=[jax.ShapeDtypeStruct((rows * ROPE_PACK, V7X_LANES), F32)] * 2,
        compiler_params=_cparams(1),
        name="rope_tables",
    )(pos_rep, inv_rep)
    return cos_t.reshape(b, s, V7X_LANES), sin_t.reshape(b, s, V7X_LANES)


def _rope_pairs(x, cos_t, sin_t):
    lane = lax.broadcasted_iota(jnp.int32, x.shape, 1)
    first_half = (lane & (SWA_DH - 1)) < (SWA_DH // 2)
    swapped = jnp.where(first_half,
                        pltpu.roll(x, V7X_LANES - SWA_DH // 2, axis=1),
                        pltpu.roll(x, SWA_DH // 2, axis=1))
    return x * cos_t + swapped * sin_t


A_QM_COL = 4 * DN_W
A_AB_COL = A_QM_COL + MEM_W
A_W_COLS = A_AB_COL + 2 * V7X_LANES
CONV_PAD = 8
CONV_COLS = 2 * V7X_LANES


def _proj_a_kernel(h_ref, w_ref, cw_ref, alog_ref, dtb_ref,
                   qkv_ref, z_ref, qm_ref, beta_ref, gc_ref, gr_ref, ext_ref, carry_ref):
    tm = h_ref.shape[1]
    lanes = V7X_LANES

    @pl.when(pl.program_id(1) == 0)
    def _():
        carry_ref[...] = jnp.zeros_like(carry_ref)

    xb = h_ref[0].astype(BF16)

    def matmul(c0, width):
        return jnp.dot(xb, w_ref[:, c0:c0 + width], preferred_element_type=F32)

    def conv_block(c0):
        for half in range(CONV_COLS // lanes):
            cc = c0 + half * lanes
            y = cw_ref[CONV_WIDTH - 1:CONV_WIDTH, cc:cc + lanes] * ext_ref[CONV_PAD:CONV_PAD + tm, cc:cc + lanes]
            for j in range(CONV_WIDTH - 1):
                r0 = CONV_PAD - (CONV_WIDTH - 1) + j
                y = y + cw_ref[j:j + 1, cc:cc + lanes] * ext_ref[r0:r0 + tm, cc:cc + lanes]
            y = y * _sigmoid(y)
            if cc < 2 * DN_W:
                y = y * lax.rsqrt(jnp.sum(y * y, axis=-1, keepdims=True) + NORM_EPS)
            qkv_ref[0, :, cc:cc + lanes] = y.astype(qkv_ref.dtype)

    starts = list(range(0, 3 * DN_W, CONV_COLS))
    ext_ref[0:CONV_PAD, :] = carry_ref[...]
    for c0 in starts:
        acc = matmul(c0, CONV_COLS)
        ext_ref[CONV_PAD:CONV_PAD + tm, c0:c0 + CONV_COLS] = acc
        carry_ref[:, c0:c0 + CONV_COLS] = acc[tm - CONV_PAD:tm, :]

    for c0 in range(0, DN_W, CONV_COLS):
        z_ref[0, :, c0:c0 + CONV_COLS] = matmul(3 * DN_W + c0, CONV_COLS).astype(z_ref.dtype)
    qm_ref[0] = matmul(A_QM_COL, MEM_W).astype(qm_ref.dtype)

    ab = matmul(A_AB_COL, 2 * lanes)
    for c0 in starts:
        conv_block(c0)
    xg = ab[:, :lanes] + dtb_ref[...]
    softplus = jnp.maximum(xg, 0.0) + jnp.log1p(jnp.exp(-jnp.abs(xg)))
    g = -jnp.exp(alog_ref[...]) * softplus
    beta_ref[0] = _sigmoid(ab[:, lanes:])

    row = lax.broadcasted_iota(jnp.int32, g.shape, 0) & (CHUNK - 1)
    gc = g
    step = 1
    while step < CHUNK:
        gc = gc + jnp.where(row >= step, pltpu.roll(gc, step, axis=0), 0.0)
        step *= 2
    gc_ref[0] = gc

    gct = jnp.transpose(gc)[0:8, :]
    plus = pltpu.roll(pltpu.roll(gct, CHUNK, axis=1), 7, axis=0)
    minus = pltpu.roll(pltpu.roll(gct, tm - CHUNK, axis=1), 1, axis=0)
    lane = lax.broadcasted_iota(jnp.int32, gct.shape, 1)
    low = (lane & (lanes - 1)) < CHUNK
    even = (lax.broadcasted_iota(jnp.int32, gct.shape, 0) & 1) == 0
    gr_ref[0] = jnp.where(even, jnp.where(low, gct, plus), jnp.where(low, minus, gct))


def _proj_a(h, w, conv_w, a_log, dt_bias, tm):
    b, s, d = h.shape
    assert s % tm == 0 and tm % V7X_LANES == 0 and w.shape == (d, A_W_COLS)
    pad = V7X_LANES - DN_HEADS
    alog = jnp.pad(a_log.astype(F32), (0, pad))[None, :]
    dtb = jnp.pad(dt_bias.astype(F32), (0, pad))[None, :]
    cur = lambda bi, i: (bi, i, 0)
    const = lambda bi, i: (0, 0)
    lanes = V7X_LANES
    return pl.pallas_call(
        _proj_a_kernel,
        grid=(b, s // tm),
        in_specs=[
            pl.BlockSpec((1, tm, d), cur),
            pl.BlockSpec((d, A_W_COLS), const),
            pl.BlockSpec((CONV_WIDTH, 3 * DN_W), const),
            pl.BlockSpec((1, lanes), const),
            pl.BlockSpec((1, lanes), const),
        ],
        out_specs=[
            pl.BlockSpec((1, tm, 3 * DN_W), cur),
            pl.BlockSpec((1, tm, DN_W), cur),
            pl.BlockSpec((1, tm, MEM_W), cur),
            pl.BlockSpec((1, tm, lanes), cur),
            pl.BlockSpec((1, tm, lanes), cur),
            pl.BlockSpec((1, 8, tm), lambda bi, i: (bi, 0, i)),
        ],
        out_shape=[
            jax.ShapeDtypeStruct((b, s, 3 * DN_W), BF16),
            jax.ShapeDtypeStruct((b, s, DN_W), BF16),
            jax.ShapeDtypeStruct((b, s, MEM_W), BF16),
            jax.ShapeDtypeStruct((b, s, lanes), F32),
            jax.ShapeDtypeStruct((b, s, lanes), F32),
            jax.ShapeDtypeStruct((b, 8, s), F32),
        ],
        scratch_shapes=[pltpu.VMEM((CONV_PAD + tm, 3 * DN_W), F32),
                        pltpu.VMEM((CONV_PAD, 3 * DN_W), F32)],
        compiler_params=_cparams(2),
        name="proj_a",
    )(h, w, conv_w.astype(F32), alog, dtb)


def _gdn_kernel(qkv_ref, z_ref, beta_ref, gc_ref, gr_ref, nw_ref, o_ref, state_ref):
    ts = qkv_ref.shape[1]
    c = CHUNK
    lanes = V7X_LANES

    @pl.when(pl.program_id(1) == 0)
    def _():
        state_ref[...] = jnp.zeros_like(state_ref)

    lane2 = lax.broadcasted_iota(jnp.int32, (c, 2 * lanes), 1)
    head_a2 = lane2 < lanes
    lane1 = lax.broadcasted_iota(jnp.int32, (c, lanes), 1)
    half_a = lane1 < c
    col = lane1 & (c - 1)
    row = lax.broadcasted_iota(jnp.int32, (c, lanes), 0)
    incl = row >= col
    strict = row > col
    eye = jnp.where(row == col, 1.0, 0.0)
    lane_sq = lax.broadcasted_iota(jnp.int32, (lanes, lanes), 1)
    half_a_sq = lane_sq < c
    nw = nw_ref[...]
    q_scale = DN_DK ** -0.5

    def split_halves(x, mask):
        return jnp.concatenate([jnp.where(mask, x, 0.0), jnp.where(mask, 0.0, x)], axis=0)

    n_sq = c.bit_length() - 1
    states = [state_ref[hd] for hd in range(DN_HEADS)]

    def process(items):
        xs, invs, intras, rhss, kdts, qes, cds = [], [], [], [], [], [], []
        for p, ci in items:
            ha, hb = 2 * p, 2 * p + 1
            r0 = ci * c
            qp = qkv_ref[0, r0:r0 + c, 2 * lanes * p:2 * lanes * (p + 1)].astype(F32)
            kp = qkv_ref[0, r0:r0 + c, DN_W + 2 * lanes * p:DN_W + 2 * lanes * (p + 1)].astype(F32)
            vp = qkv_ref[0, r0:r0 + c, 2 * DN_W + 2 * lanes * p:2 * DN_W + 2 * lanes * (p + 1)].astype(F32)
            beta_t = beta_ref[0, r0:r0 + c, :]
            gc_t = gc_ref[0, r0:r0 + c, :]
            b_a, b_b = beta_t[:, ha:ha + 1], beta_t[:, hb:hb + 1]
            g_a, g_b = gc_t[:, ha:ha + 1], gc_t[:, hb:hb + 1]
            beta2 = jnp.where(head_a2, b_a, b_b)
            g2 = jnp.where(head_a2, g_a, g_b)
            g1 = jnp.where(half_a, g_a, g_b)
            g_row = gr_ref[0, 2 * p + (ci % 2):2 * p + (ci % 2) + 1, lanes * (ci // 2):lanes * (ci // 2 + 1)]
            g_last = g2[c - 1:c, :]
            eg = jnp.exp(g2)
            kb = kp * beta2
            vb = vp * beta2
            kbe = kb * eg
            qs = qp * q_scale
            kd = kp * jnp.exp(g_last - g2)
            qes.append(qs * eg)
            cds.append(jnp.exp(g_last))

            r1 = _mm_nt(jnp.concatenate([kb, qs], axis=0), split_halves(kp, head_a2))
            decay = jnp.where(incl, jnp.exp(g1 - g_row), 0.0)
            x = jnp.where(strict, -(r1[:c] * decay), 0.0)
            xs.append(x)
            invs.append(eye + x)
            intras.append(r1[c:] * decay)
            rhss.append(jnp.concatenate([
                jnp.concatenate([vb[:, :lanes], kbe[:, :lanes]], axis=1),
                jnp.concatenate([vb[:, lanes:], kbe[:, lanes:]], axis=1)], axis=0).astype(BF16))
            kdts.append(jnp.transpose(jnp.concatenate([kd[:, :lanes], kd[:, lanes:]], axis=0)))

        xs = [_mm(x, split_halves(x, half_a)) for x in xs]
        for _ in range(n_sq - 2):
            rs = [_mm(jnp.concatenate([x, inv], axis=0), split_halves(x, half_a)) for x, inv in zip(xs, invs)]
            xs = [r[:c] for r in rs]
            invs = [inv + r[c:] for inv, r in zip(invs, rs)]
        invs = [inv + _mm(inv, split_halves(x, half_a)) for x, inv in zip(xs, invs)]

        sols = [_mm(split_halves(inv, half_a), rhs) for inv, rhs in zip(invs, rhss)]
        r3s = [_mm(jnp.concatenate([split_halves(intra, half_a), split_halves(kdt, half_a_sq)], axis=0), sol)
               for intra, kdt, sol in zip(intras, kdts, sols)]

        for (p, ci), r3, qe, cd in zip(items, r3s, qes, cds):
            r0 = ci * c
            for hx in range(2):
                hd = 2 * p + hx
                lo = hx * lanes
                s_x = states[hd]
                o0 = r3[hx * c:(hx + 1) * c, :lanes]
                iw = r3[hx * c:(hx + 1) * c, lanes:]
                n0 = r3[2 * c + hx * lanes:2 * c + (hx + 1) * lanes, :lanes]
                kw = r3[2 * c + hx * lanes:2 * c + (hx + 1) * lanes, lanes:]
                r4 = _mm(jnp.concatenate([qe[:, lo:lo + lanes] - iw, -kw], axis=0), s_x)
                o = r4[:c] + o0
                states[hd] = cd[:, lo:lo + lanes] * s_x + n0 + r4[c:]
                zt = z_ref[0, r0:r0 + c, hd * lanes:(hd + 1) * lanes].astype(F32)
                on = o * lax.rsqrt(jnp.mean(o * o, axis=-1, keepdims=True) + NORM_EPS) * nw
                o_ref[0, r0:r0 + c, hd * lanes:(hd + 1) * lanes] = (on * (zt * _sigmoid(zt))).astype(o_ref.dtype)

    process([(p, ci) for ci in range(ts // c) for p in range(DN_PAIRS)])
    for hd in range(DN_HEADS):
        state_ref[hd] = states[hd]


def _gdn(qkv, z, beta, gc, gr, norm_w, ts):
    b, s, w = qkv.shape
    return pl.pallas_call(
        _gdn_kernel,
        grid=(b, s // ts),
        in_specs=[
            pl.BlockSpec((1, ts, w), lambda bi, i: (bi, i, 0)),
            pl.BlockSpec((1, ts, DN_W), lambda bi, i: (bi, i, 0)),
            pl.BlockSpec((1, ts, V7X_LANES), lambda bi, i: (bi, i, 0)),
            pl.BlockSpec((1, ts, V7X_LANES), lambda bi, i: (bi, i, 0)),
            pl.BlockSpec((1, 8, ts), lambda bi, i: (bi, 0, i)),
            pl.BlockSpec((1, V7X_LANES), lambda bi, i: (0, 0)),
        ],
        out_specs=pl.BlockSpec((1, ts, DN_W), lambda bi, i: (bi, i, 0)),
        out_shape=jax.ShapeDtypeStruct((b, s, DN_W), BF16),
        scratch_shapes=[pltpu.VMEM((DN_HEADS, DN_DK, DN_DK), F32)],
        compiler_params=_cparams(2),
        name="gdn",
    )(qkv, z, beta, gc, gr, norm_w.astype(F32)[None, :])


SWA_Q_SCALE = SWA_DH ** -0.5 * LOG2_E


def _proj_b_kernel(h_ref, w_ref, cos_ref, sin_ref, qt_ref, qm_ref):
    lanes = V7X_LANES
    xb = h_ref[0].astype(BF16)
    cos_t = cos_ref[0]
    sin_t = sin_ref[0]
    for c0 in range(0, SWA_Q_W, 2 * lanes):
        acc = jnp.dot(xb, w_ref[:, c0:c0 + 2 * lanes], preferred_element_type=F32)
        for half in range(2):
            q = _rope_pairs(acc[:, half * lanes:(half + 1) * lanes], cos_t, sin_t) * SWA_Q_SCALE
            qt_ref[0, c0 // lanes + half] = jnp.transpose(q).astype(BF16)
    qm_ref[0] = jnp.dot(xb, w_ref[:, SWA_Q_W:], preferred_element_type=F32).astype(qm_ref.dtype)


def _proj_b(h, w, layer, cos_t, sin_t, tm):
    b, s, d = h.shape
    lanes = V7X_LANES
    cur = lambda bi, i: (bi, i, 0)
    return pl.pallas_call(
        _proj_b_kernel,
        grid=(b, s // tm),
        in_specs=[
            pl.BlockSpec((1, tm, d), cur),
            pl.BlockSpec((None,) + w.shape[1:], lambda bi, i: (layer, 0, 0)),
            pl.BlockSpec((1, tm, lanes), cur),
            pl.BlockSpec((1, tm, lanes), cur),
        ],
        out_specs=[
            pl.BlockSpec((1, SWA_PAIRS, lanes, tm), lambda bi, i: (bi, 0, 0, i)),
            pl.BlockSpec((1, tm, MEM_W), cur),
        ],
        out_shape=[
            jax.ShapeDtypeStruct((b, SWA_PAIRS, lanes, s), BF16),
            jax.ShapeDtypeStruct((b, s, MEM_W), BF16),
        ],
        compiler_params=_cparams(2),
        name="proj_b",
    )(h, w, cos_t, sin_t)


def _emit_shared_kv(h_tile, w_ref, cos_t, sin_t, kvar_ref, vt_ref):
    lanes = V7X_LANES
    acc = jnp.dot(h_tile.astype(BF16), w_ref[...], preferred_element_type=F32)
    k_rot = _rope_pairs(acc[:, :lanes], cos_t, sin_t)
    v = acc[:, lanes:]
    lane = lax.broadcasted_iota(jnp.int32, k_rot.shape, 1)
    low = lane < SWA_DH
    k_sw = pltpu.roll(k_rot, SWA_DH, axis=1)
    kvar_ref[0, 0] = jnp.where(low, k_rot, 0.0).astype(BF16)
    kvar_ref[0, 1] = jnp.where(low, 0.0, k_sw).astype(BF16)
    kvar_ref[0, 2] = jnp.where(low, k_sw, 0.0).astype(BF16)
    kvar_ref[0, 3] = jnp.where(low, 0.0, k_rot).astype(BF16)
    v_sw = pltpu.roll(v, SWA_DH, axis=1)
    ones_hi = jnp.where(lane == SWA_DH, 1.0, 0.0)
    ones_lo = jnp.where(lane == 0, 1.0, 0.0)
    vt_ref[0, 0] = jnp.transpose(jnp.where(low, v, ones_hi)).astype(BF16)
    vt_ref[0, 1] = jnp.transpose(jnp.where(low, ones_lo, v_sw)).astype(BF16)
    vt_ref[0, 2] = jnp.transpose(jnp.where(low, v_sw, ones_hi)).astype(BF16)
    vt_ref[0, 3] = jnp.transpose(jnp.where(low, ones_lo, v)).astype(BF16)


def _swa_kernel(sink_ref, q_ref, kv_ref, kvh_ref, vt_ref, vth_ref, o_ref):
    tq = q_ref.shape[3]
    lanes = V7X_LANES
    w = WINDOW
    first_key = jnp.where(pl.program_id(1) == 0, w, 0)
    sum_row = (SWA_DH, 0)

    def key_rows(var, j):
        if j == 0:
            return jnp.concatenate([kvh_ref[0, var], kv_ref[0, var, 0:w, :]], axis=0)
        return kv_ref[0, var, (j - 1) * w:(j + 1) * w, :]

    def value_cols(var, j):
        if j == 0:
            return jnp.concatenate([vth_ref[0, var], vt_ref[0, var, :, 0:w]], axis=1)
        return vt_ref[0, var, :, (j - 1) * w:(j + 1) * w]

    cols3 = SWA_PAIRS_PER_KV * w
    kj = lax.broadcasted_iota(jnp.int32, (2 * w, cols3), 0)
    qi = lax.broadcasted_iota(jnp.int32, (2 * w, cols3), 1) & (w - 1)
    dist = qi + w - kj
    band = (dist >= 0) & (dist < w)
    neg_inf = jnp.float32(-jnp.inf)
    bias_band = jnp.where(band, 0.0, neg_inf)
    bias_first = jnp.where(band & (kj >= first_key), 0.0, neg_inf)
    head_slot = lax.broadcasted_iota(jnp.int32, (1, cols3), 1) // w
    row_low = lax.broadcasted_iota(jnp.int32, (lanes, cols3), 0) < SWA_DH
    combos = [(g, e) for g in range(SWA_KV_HEADS) for e in range(2)]
    group_pairs = [[SWA_PAIRS_PER_KV * g + t for t in range(SWA_PAIRS_PER_KV)] for g in range(SWA_KV_HEADS)]

    def scores(j):
        r0 = j * w
        out = []
        for g, e in combos:
            qs = jnp.concatenate([q_ref[0, p, :, r0:r0 + w] for p in group_pairs[g]], axis=1)
            out.append(jnp.dot(key_rows(2 * g + e, j), qs, preferred_element_type=F32))
        return out

    def finish(j, ss):
        r0 = j * w
        bias = bias_first if j == 0 else bias_band
        pes, sink_terms = [], []
        for (g, e), s in zip(combos, ss):
            s = s + bias
            hs = [2 * p + e for p in group_pairs[g]]
            sink = jnp.where(head_slot == 0, sink_ref[hs[0]],
                             jnp.where(head_slot == 1, sink_ref[hs[1]], sink_ref[hs[2]])) * LOG2_E
            m = jnp.maximum(jnp.max(s, axis=0, keepdims=True), sink)
            pes.append(jnp.exp2(s - m).astype(BF16))
            sink_terms.append(jnp.exp2(sink - m))
        pvs = [jnp.dot(value_cols(2 * g + e, j), pe, preferred_element_type=F32)
               for (g, e), pe in zip(combos, pes)]
        for g in range(SWA_KV_HEADS):
            normed = []
            for e in range(2):
                pv = pvs[2 * g + e]
                den = pv[sum_row[e]:sum_row[e] + 1, :] + sink_terms[2 * g + e]
                normed.append(pv * (1.0 / den))
            acc = jnp.where(row_low, normed[0], normed[1])
            for t, p in enumerate(group_pairs[g]):
                o_ref[0, r0:r0 + w, p * lanes:(p + 1) * lanes] = (
                    jnp.transpose(acc[:, t * w:(t + 1) * w]).astype(o_ref.dtype))

    n_blocks = tq // w
    ss = scores(0)
    for j in range(n_blocks):
        nxt = scores(j + 1) if j + 1 < n_blocks else None
        finish(j, ss)
        ss = nxt


def _swa(q_t, kvar, vt, sinks, tq):
    b, _, lanes, s = q_t.shape
    assert s % tq == 0 and tq % WINDOW == 0
    hb = tq // WINDOW
    prev = lambda i: jnp.maximum(i * hb - 1, 0)
    return pl.pallas_call(
        _swa_kernel,
        grid=(b, s // tq),
        in_specs=[
            pl.BlockSpec(memory_space=pltpu.SMEM),
            pl.BlockSpec((1, SWA_PAIRS, lanes, tq), lambda bi, i: (bi, 0, 0, i)),
            pl.BlockSpec((1, 4, tq, lanes), lambda bi, i: (bi, 0, i, 0)),
            pl.BlockSpec((1, 4, WINDOW, lanes), lambda bi, i: (bi, 0, prev(i), 0)),
            pl.BlockSpec((1, 4, lanes, tq), lambda bi, i: (bi, 0, 0, i)),
            pl.BlockSpec((1, 4, lanes, WINDOW), lambda bi, i: (bi, 0, 0, prev(i))),
        ],
        out_specs=pl.BlockSpec((1, tq, SWA_Q_W), lambda bi, i: (bi, i, 0)),
        out_shape=jax.ShapeDtypeStruct((b, s, SWA_Q_W), BF16),
        compiler_params=_cparams(2),
        name="swa",
    )(sinks.astype(F32), q_t, kvar, kvar, vt, vt)


def _mix_out_kernel(o_ref, qm_ref, mk_ref, mvt_ref, wo_ref, h_ref, g_ref, b_ref, out_ref):
    mk = mk_ref[0].astype(F32)
    mvt = mvt_ref[0]
    lane_head = lax.broadcasted_iota(jnp.int32, mk.shape, 1) // MEM_DH
    heads = range(MEM_HEADS)
    khs = [jnp.where(lane_head == hd, mk, 0.0).astype(BF16) for hd in heads]
    ones = jnp.ones((BF16_SUBLANES, mvt.shape[1]), BF16)
    vhs = [jnp.concatenate([mvt[hd * MEM_DH:(hd + 1) * MEM_DH, :], ones], axis=0) for hd in heads]
    dn = o_ref.shape[2]
    qm = qm_ref[0].astype(F32) * (MEM_DH ** -0.5 * LOG2_E)
    qm_t = jnp.transpose(qm).astype(BF16)
    ss = [jnp.dot(kh, qm_t, preferred_element_type=F32) for kh in khs]
    half = qm_t.shape[1] // 2
    mix_a = jnp.dot(o_ref[0, 0:half, :], wo_ref[0:dn, :], preferred_element_type=F32)
    pes = [jnp.exp2(s - jnp.max(s, axis=0, keepdims=True)).astype(BF16) for s in ss]
    parts = []
    for vh, pe in zip(vhs, pes):
        pv = jnp.dot(vh, pe, preferred_element_type=F32)
        parts.append(pv[:MEM_DH] * (1.0 / pv[MEM_DH:MEM_DH + 1]))
    mix_b = jnp.dot(o_ref[0, half:, :], wo_ref[0:dn, :], preferred_element_type=F32)
    mix = jnp.concatenate([mix_a, mix_b], axis=0)
    mo = jnp.transpose(jnp.concatenate(parts, axis=0))
    mix = mix + jnp.dot(mo.astype(BF16), wo_ref[dn:, :], preferred_element_type=F32)
    out_ref[0] = _layer_norm(DN_ALPHA * h_ref[0] + mix, g_ref[...], b_ref[...])


def _mix_out(o, qm, mk, mvt, wo, layer, h, ln_g, ln_b, tm):
    b, s, d = h.shape
    m_tok = mk.shape[1]
    cur = lambda bi, i: (bi, i, 0)
    per_b = lambda bi, i: (bi, 0, 0)
    const = lambda bi, i: (0, 0)
    return pl.pallas_call(
        _mix_out_kernel,
        grid=(b, s // tm),
        in_specs=[
            pl.BlockSpec((1, tm, o.shape[2]), cur),
            pl.BlockSpec((1, tm, MEM_W), cur),
            pl.BlockSpec((1, m_tok, MEM_W), per_b),
            pl.BlockSpec((1, MEM_W, m_tok), per_b),
            pl.BlockSpec((None,) + wo.shape[1:], lambda bi, i: (layer, 0, 0)),
            pl.BlockSpec((1, tm, d), cur),
            pl.BlockSpec((1, d), const),
            pl.BlockSpec((1, d), const),
        ],
        out_specs=pl.BlockSpec((1, tm, d), cur),
        out_shape=jax.ShapeDtypeStruct((b, s, d), F32),
        compiler_params=_cparams(2),
        name="mix_out",
    )(o, qm, mk, mvt, wo, h, ln_g[None, :], ln_b[None, :])


def _mlp_kernel(x_ref, up_ref, dn_ref, g_ref, b_ref, *rest, hid_chunk, with_kv):
    o_ref = rest[3] if with_kv else rest[0]
    x = x_ref[...]
    xb = x.astype(BF16)
    hidden = up_ref.shape[1]
    acc = DN_ALPHA * x
    for j in range(0, hidden, hid_chunk):
        hid = jnp.dot(xb, up_ref[:, j:j + hid_chunk], preferred_element_type=F32)
        hid = jnp.square(jnp.maximum(hid, 0.0))
        acc = acc + jnp.dot(hid.astype(BF16), dn_ref[j:j + hid_chunk, :], preferred_element_type=F32)
    out = _layer_norm(acc, g_ref[...], b_ref[...])
    o_ref[...] = out
    if with_kv:
        wkv_ref, cos_ref, sin_ref, _, kvar_ref, vt_ref = rest
        _emit_shared_kv(out, wkv_ref, cos_ref[0], sin_ref[0], kvar_ref, vt_ref)


def _mlp(x, w_up, w_down, layer, ln_g, ln_b, tm, shared_kv=None):
    m, d = x.shape
    hidden = w_up.shape[2]
    lanes = V7X_LANES
    in_specs = [
        pl.BlockSpec((tm, d), lambda i: (i, 0)),
        pl.BlockSpec((None, d, hidden), lambda i: (layer, 0, 0), pipeline_mode=pl.Buffered(1)),
        pl.BlockSpec((None, hidden, d), lambda i: (layer, 0, 0), pipeline_mode=pl.Buffered(1)),
        pl.BlockSpec((1, d), lambda i: (0, 0)),
        pl.BlockSpec((1, d), lambda i: (0, 0)),
    ]
    out_specs = [pl.BlockSpec((tm, d), lambda i: (i, 0))]
    out_shape = [jax.ShapeDtypeStruct((m, d), F32)]
    args = [x, w_up, w_down, ln_g[None, :], ln_b[None, :]]
    if shared_kv is not None:
        w_kv, cos_t, sin_t = shared_kv
        b, s, _ = cos_t.shape
        nb = s // tm
        assert b * s == m and s % tm == 0
        in_specs += [pl.BlockSpec(w_kv.shape, lambda i: (0, 0)),
                     pl.BlockSpec((1, tm, lanes), lambda i: (i // nb, i % nb, 0)),
                     pl.BlockSpec((1, tm, lanes), lambda i: (i // nb, i % nb, 0))]
        out_specs += [pl.BlockSpec((1, 4, tm, lanes), lambda i: (i // nb, 0, i % nb, 0)),
                      pl.BlockSpec((1, 4, lanes, tm), lambda i: (i // nb, 0, 0, i % nb))]
        out_shape += [jax.ShapeDtypeStruct((b, 4, s, lanes), BF16),
                      jax.ShapeDtypeStruct((b, 4, lanes, s), BF16)]
        args += [w_kv, cos_t, sin_t]
    return pl.pallas_call(
        functools.partial(_mlp_kernel, hid_chunk=1024, with_kv=shared_kv is not None),
        grid=(m // tm,),
        in_specs=in_specs,
        out_specs=out_specs,
        out_shape=out_shape,
        compiler_params=_cparams(1),
        name="mlp",
    )(*args)


def _a_weights_kernel(w_ref, o_ref):
    lanes = V7X_LANES
    ab0 = A_QM_COL
    o_ref[:, 0:A_QM_COL] = w_ref[:, 0:A_QM_COL].astype(BF16)
    o_ref[:, A_QM_COL:A_AB_COL] = w_ref[:, ab0 + 2 * DN_HEADS:ab0 + 2 * DN_HEADS + MEM_W].astype(BF16)
    first = w_ref[:, ab0:ab0 + lanes]
    lane = lax.broadcasted_iota(jnp.int32, first.shape, 1)
    a = jnp.where(lane < DN_HEADS, first, 0.0)
    b = jnp.where(lane < DN_HEADS, pltpu.roll(first, lanes - DN_HEADS, axis=1), 0.0)
    o_ref[:, A_AB_COL:A_AB_COL + lanes] = a.astype(BF16)
    o_ref[:, A_AB_COL + lanes:] = b.astype(BF16)


def _a_proj_weights(w_in_all, layer):
    _, k, n = w_in_all.shape
    assert n == A_QM_COL + 2 * DN_HEADS + MEM_W
    tk = 256
    return pl.pallas_call(
        _a_weights_kernel,
        grid=(k // tk,),
        in_specs=[pl.BlockSpec((None, tk, n), lambda i: (layer, i, 0))],
        out_specs=pl.BlockSpec((tk, A_W_COLS), lambda i: (i, 0)),
        out_shape=jax.ShapeDtypeStruct((k, A_W_COLS), BF16),
        compiler_params=_cparams(1),
        name="a_weights",
    )(w_in_all)


def _tile(n, pref):
    t = min(n, pref)
    assert n % t == 0
    return t


def kernel(x, mem, positions, a_w_in, a_conv_w, a_A_log, a_dt_bias, a_norm_w, b_w_in, b_sinks,
           w_kv_shared, mem_w_kv, w_o, mlp_w_up, mlp_w_down, ln_g, ln_b):
    b, s, d = x.shape
    t = b * s
    m_tok = mem.shape[1]
    tm = _tile(s, ROW_TILE)
    mem2 = mem.reshape(b * m_tok, d)
    h = x
    cos_t = sin_t = kvar = vt = None
    w_o_b, b_w_in_b = w_o.astype(BF16), b_w_in.astype(BF16)
    w_up_b, w_down_b = mlp_w_up.astype(BF16), mlp_w_down.astype(BF16)
    for layer in range(DEPTH):
        mk, mv = _proj(mem2, mem_w_kv[layer].astype(BF16), ((0, MEM_W), (MEM_W, MEM_W)), (BF16, BF16),
                       b * m_tok)
        mk = mk.reshape(b, m_tok, MEM_W)
        mvt = jnp.swapaxes(mv.reshape(b, m_tok, MEM_W), 1, 2)
        if layer < N_A:
            qkv, z, qm, beta, gc, gr = _proj_a(h, _a_proj_weights(a_w_in, layer), a_conv_w[layer],
                                               a_A_log[layer], a_dt_bias[layer], tm)
            ts = _tile(s, GDN_TILE)
            o = _gdn(qkv, z, beta, gc, gr, a_norm_w[layer], ts)
        else:
            j = layer - N_A
            q_t, qm = _proj_b(h, b_w_in_b, j, cos_t, sin_t, tm)
            o = _swa(q_t, kvar, vt, b_sinks[j], _tile(s, SWA_TILE))
        h = _mix_out(o, qm, mk, mvt, w_o_b, layer, h, ln_g[layer, 0], ln_b[layer, 0], tm)
        if layer == N_A - 1:
            cos_t, sin_t = _rope_tables(positions)
            h, kvar, vt = _mlp(h.reshape(t, d), w_up_b, w_down_b, layer, ln_g[layer, 1], ln_b[layer, 1], tm,
                               shared_kv=(w_kv_shared.astype(BF16), cos_t, sin_t))
        else:
            (h,) = _mlp(h.reshape(t, d), w_up_b, w_down_b, layer, ln_g[layer, 1], ln_b[layer, 1], tm)
        h = h.reshape(b, s, d)
    return h
```
